```python
import jax, jax.numpy as jnp
from jax import lax
import numpy as np

D_MODEL = 1024
BATCH = 32
SEQ = 2048
DEPTH = 1

HEAD_DIM = 64
N_Q_HEADS = 8
N_KV_HEADS = 2
Q_PER_KV = N_Q_HEADS // N_KV_HEADS
ATTN_WIDTH = N_Q_HEADS * HEAD_DIM
KV_WIDTH = N_KV_HEADS * HEAD_DIM
WINDOW = 128
BLOCK = WINDOW
ROPE_THETA = 10000.0
POOL_WINDOWS = (2, 4, 8, 16)
N_POOL_GROUPS = len(POOL_WINDOWS)
POOL_WIDTH = D_MODEL - ATTN_WIDTH
POOL_GROUP_DIM = POOL_WIDTH // N_POOL_GROUPS
MIX_WIDTH = ATTN_WIDTH + POOL_WIDTH
IN_WIDTH = ATTN_WIDTH + 2 * KV_WIDTH + POOL_WIDTH
D_FF = 4 * D_MODEL
EPS = 1e-6

kernel_name = "hymba_swa_sink_multiscale_pool_block"


def _rmsnorm(x, g):
    xf = x.astype(jnp.float32)
    y = xf * lax.rsqrt(jnp.mean(xf * xf, axis=-1, keepdims=True) + EPS)
    return (y * g.astype(jnp.float32)).astype(x.dtype)


def _rope(x, pos):
    half = HEAD_DIM // 2
    inv_freq = ROPE_THETA ** (-jnp.arange(half, dtype=jnp.float32) / half)
    ang = pos.astype(jnp.float32)[:, None] * inv_freq[None, :]
    cos = jnp.cos(ang)[None, :, None, :]
    sin = jnp.sin(ang)[None, :, None, :]
    xf = x.astype(jnp.float32)
    x1, x2 = xf[..., :half], xf[..., half:]
    out = jnp.concatenate([x1 * cos - x2 * sin, x2 * cos + x1 * sin], axis=-1)
    return out.astype(x.dtype)


def _swa_with_sinks(q, k, v, sinks):
    B, S = q.shape[0], q.shape[1]
    nb = S // BLOCK
    qb = q.reshape(B, nb, BLOCK, N_KV_HEADS, Q_PER_KV, HEAD_DIM)
    kb = k.reshape(B, nb, BLOCK, N_KV_HEADS, HEAD_DIM)
    vb = v.reshape(B, nb, BLOCK, N_KV_HEADS, HEAD_DIM)

    def with_prev(t):
        prev = jnp.pad(t[:, :-1], ((0, 0), (1, 0), (0, 0), (0, 0), (0, 0)))
        return jnp.concatenate([prev, t], axis=2)

    kw, vw = with_prev(kb), with_prev(vb)
    scale = HEAD_DIM ** -0.5
    logits = jnp.einsum('bnqkgd,bnskd->bnkgqs', qb, kw,
                        preferred_element_type=jnp.float32) * scale
    blk = jnp.arange(nb)[:, None, None]
    qi = jnp.arange(BLOCK)[None, :, None]
    kj = jnp.arange(2 * BLOCK)[None, None, :]
    rel = BLOCK + qi - kj
    kpos = (blk - 1) * BLOCK + kj
    mask = (rel >= 0) & (rel < WINDOW) & (kpos >= 0)
    logits = jnp.where(mask[None, :, None, None], logits, -jnp.inf)
    sink = sinks.astype(jnp.float32).reshape(N_KV_HEADS, Q_PER_KV)[None, None, :, :, None, None]
    m = jnp.maximum(jnp.max(logits, axis=-1, keepdims=True), sink)
    p = jnp.exp(logits - m)
    denom = jnp.sum(p, axis=-1, keepdims=True) + jnp.exp(sink - m)
    probs = (p / denom).astype(v.dtype)
    out = jnp.einsum('bnkgqs,bnskd->bnqkgd', probs, vw)
    return out.reshape(B, S, ATTN_WIDTH)


def _multiscale_pool(u, w_pool, pool_scale):
    B, S = u.shape[0], u.shape[1]
    ug = u.reshape(B, S, N_POOL_GROUPS, POOL_GROUP_DIM).astype(jnp.float32)
    c = jnp.pad(jnp.cumsum(ug, axis=1), ((0, 0), (1, 0), (0, 0), (0, 0)))
    t = jnp.arange(S)
    means = []
    for g, w in enumerate(POOL_WINDOWS):
        cg = c[:, :, g]
        lagged = jnp.pad(cg[:, :S + 1 - w], ((0, 0), (w - 1, 0), (0, 0)))
        cnt = jnp.minimum(t + 1, w).astype(jnp.float32)[None, :, None]
        means.append((cg[:, 1:] - lagged) / cnt)
    mean = jnp.stack(means, axis=2)
    d = (mean - ug).astype(u.dtype)
    y = jnp.einsum('bsgc,gcd->bsgd', d, w_pool)
    return y.reshape(B, S, POOL_WIDTH) * pool_scale


def setup_inputs(seed: int = 0) -> dict:
    key = jax.random.key(seed)
    ks = jax.random.split(key, 12)
    f32 = jnp.float32
    x = jax.random.normal(ks[0], (BATCH, SEQ, D_MODEL), f32)
    attn_norm_g = 1.0 + 0.02 * jax.random.normal(ks[1], (DEPTH, D_MODEL), f32)
    w_in = jax.random.normal(ks[2], (DEPTH, D_MODEL, IN_WIDTH), f32) * D_MODEL ** -0.5
    attn_sinks = 0.5 * jax.random.normal(ks[3], (DEPTH, N_Q_HEADS), f32)
    w_pool = jax.random.normal(ks[4], (DEPTH, N_POOL_GROUPS, POOL_GROUP_DIM, POOL_GROUP_DIM), f32) * POOL_GROUP_DIM ** -0.5
    pool_scale = 1.0 + 0.1 * jax.random.normal(ks[5], (DEPTH, POOL_WIDTH), f32)
    w_out = jax.random.normal(ks[6], (DEPTH, MIX_WIDTH, D_MODEL), f32) * MIX_WIDTH ** -0.5
    mlp_norm_g = 1.0 + 0.02 * jax.random.normal(ks[7], (DEPTH, D_MODEL), f32)
    w_up = jax.random.normal(ks[8], (DEPTH, D_MODEL, D_FF), f32) * D_MODEL ** -0.5
    w_down = jax.random.normal(ks[9], (DEPTH, D_FF, D_MODEL), f32) * D_FF ** -0.5
    final_norm_g = 1.0 + 0.02 * jax.random.normal(ks[10], (D_MODEL,), f32)
    return {"x": x, "attn_norm_g": attn_norm_g, "w_in": w_in, "attn_sinks": attn_sinks,
            "w_pool": w_pool, "pool_scale": pool_scale, "w_out": w_out,
            "mlp_norm_g": mlp_norm_g, "w_up": w_up, "w_down": w_down,
            "final_norm_g": final_norm_g}


def reference(x, attn_norm_g, w_in, attn_sinks, w_pool, pool_scale, w_out,
              mlp_norm_g, w_up, w_down, final_norm_g):
    B, S = x.shape[0], x.shape[1]
    pos = jnp.arange(S)
    for l in range(DEPTH):
        h = _rmsnorm(x, attn_norm_g[l])
        proj = h @ w_in[l]
        q = proj[..., :ATTN_WIDTH].reshape(B, S, N_Q_HEADS, HEAD_DIM)
        k = proj[..., ATTN_WIDTH:ATTN_WIDTH + KV_WIDTH].reshape(B, S, N_KV_HEADS, HEAD_DIM)
        v = proj[..., ATTN_WIDTH + KV_WIDTH:ATTN_WIDTH + 2 * KV_WIDTH].reshape(B, S, N_KV_HEADS, HEAD_DIM)
        u = proj[..., ATTN_WIDTH + 2 * KV_WIDTH:]
        q, k = _rope(q, pos), _rope(k, pos)
        attn = _swa_with_sinks(q, k, v, attn_sinks[l])
        pool = _multiscale_pool(u, w_pool[l], pool_scale[l])
        x = x + jnp.concatenate([attn, pool], axis=-1) @ w_out[l]
        h = _rmsnorm(x, mlp_norm_g[l])
        x = x + jnp.square(jax.nn.relu(h @ w_up[l])) @ w_down[l]
    return _rmsnorm(x, final_norm_g)
```

```python
import functools

import jax
import jax.numpy as jnp
from jax import lax
from jax.experimental import pallas as pl
from jax.experimental.pallas import tpu as pltpu

D_MODEL = 1024
HEAD_DIM = 64
N_Q_HEADS = 8
N_KV_HEADS = 2
Q_PER_KV = N_Q_HEADS // N_KV_HEADS
ATTN_WIDTH = N_Q_HEADS * HEAD_DIM
KV_WIDTH = N_KV_HEADS * HEAD_DIM
WINDOW = 128
BLOCK = WINDOW
ROPE_THETA = 10000.0
POOL_WINDOWS = (2, 4, 8, 16)
N_POOL_GROUPS = len(POOL_WINDOWS)
POOL_WIDTH = D_MODEL - ATTN_WIDTH
POOL_GROUP_DIM = POOL_WIDTH // N_POOL_GROUPS
POOL_HALO = 16
IN_WIDTH = ATTN_WIDTH + 2 * KV_WIDTH + POOL_WIDTH
D_FF = 4 * D_MODEL
EPS = 1e-6

LANES = 128
SEQ_TILE = 512
FF_CHUNK = 1024
VMEM_LIMIT_BYTES = 56 * 1024 * 1024

F32 = jnp.float32
BF16 = jnp.bfloat16


def _rmsnorm(x, g):
    ms = jnp.mean(x * x, axis=-1, keepdims=True)
    return x * lax.rsqrt(ms + EPS) * g


def _dot(a, b):
    return jnp.dot(a, b, preferred_element_type=F32)


def _rope(xc, cos, sin_signed, first_half):
    swapped = jnp.where(first_half, pltpu.roll(xc, LANES - HEAD_DIM // 2, 1),
                        pltpu.roll(xc, HEAD_DIM // 2, 1))
    return xc * cos + swapped * sin_signed


def _dup_heads(t, low_half):
    r = pltpu.roll(t, HEAD_DIM, 1)
    return jnp.where(low_half, t, r), jnp.where(low_half, r, t)


def _block_kernel(sink_ref, x_ref, cos_ref, sin_ref, g1_ref, win_ref, wpool_ref, pscale_ref,
                  wout_ref, g2_ref, wup_ref, wdown_ref, gf_ref, o_ref,
                  kprev_ref, vprev_ref, uprev_ref, mix_ref):
    T = SEQ_TILE
    s = pl.program_id(1)

    @pl.when(s == 0)
    def _():
        kprev_ref[...] = jnp.zeros_like(kprev_ref)
        vprev_ref[...] = jnp.zeros_like(vprev_ref)
        uprev_ref[...] = jnp.zeros_like(uprev_ref)

    x = x_ref[...]
    h = _rmsnorm(x, g1_ref[...]).astype(BF16)
    proj = _dot(h, win_ref[...])

    lane = lax.broadcasted_iota(jnp.int32, (1, LANES), 1)
    low_half = lane < HEAD_DIM
    first_half = (lane % HEAD_DIM) < (HEAD_DIM // 2)
    cos = cos_ref[...]
    sin_signed = sin_ref[...]

    scale = HEAD_DIM ** -0.5
    q_cols = []
    for c in range(ATTN_WIDTH // LANES):
        qc = proj[:, c * LANES:(c + 1) * LANES]
        q_cols.append(_rope(qc, cos, sin_signed, first_half) * scale)
    k_r = _rope(proj[:, ATTN_WIDTH:ATTN_WIDTH + KV_WIDTH], cos, sin_signed, first_half)
    v_r = proj[:, ATTN_WIDTH + KV_WIDTH:ATTN_WIDTH + 2 * KV_WIDTH]
    k_dup = [t.astype(BF16) for t in _dup_heads(k_r, low_half)]
    v_dup = [t.astype(BF16) for t in _dup_heads(v_r, low_half)]
    k_ext = [jnp.concatenate([kprev_ref[g], k_dup[g]], axis=0) for g in range(N_KV_HEADS)]
    v_ext = [jnp.concatenate([vprev_ref[g], v_dup[g]], axis=0) for g in range(N_KV_HEADS)]
    for g in range(N_KV_HEADS):
        kprev_ref[g] = k_dup[g][T - BLOCK:, :]
        vprev_ref[g] = v_dup[g][T - BLOCK:, :]

    qi = lax.broadcasted_iota(jnp.int32, (BLOCK, 2 * BLOCK), 0)
    kj = lax.broadcasted_iota(jnp.int32, (BLOCK, 2 * BLOCK), 1)
    rel = BLOCK + qi - kj
    band = (rel >= 0) & (rel < WINDOW)
    band_first = band & ((kj >= BLOCK) | (s > 0))

    for j in range(T // BLOCK):
        rows = slice(j * BLOCK, (j + 1) * BLOCK)
        mask1 = band_first if j == 0 else band
        mask = jnp.concatenate([mask1] * Q_PER_KV, axis=0)
        heads_out = []
        for g in range(N_KV_HEADS):
            q_parts, sink_parts = [], []
            for i in range(Q_PER_KV):
                hd = g * Q_PER_KV + i
                qc = q_cols[hd // 2][rows, :]
                keep = low_half if hd % 2 == 0 else jnp.logical_not(low_half)
                q_parts.append(jnp.where(keep, qc, 0.0).astype(BF16))
                sink_parts.append(jnp.full((BLOCK, 1), sink_ref[hd], F32))
            q_stack = jnp.concatenate(q_parts, axis=0)
            sink = jnp.concatenate(sink_parts, axis=0)
            k_win = k_ext[g][j * BLOCK:(j + 2) * BLOCK, :]
            v_win = v_ext[g][j * BLOCK:(j + 2) * BLOCK, :]
            logits = lax.dot_general(q_stack, k_win, (((1,), (1,)), ((), ())),
                                     preferred_element_type=F32)
            logits = jnp.where(mask, logits, -jnp.inf)
            m = jnp.maximum(jnp.max(logits, axis=-1, keepdims=True), sink)
            p = jnp.exp(logits - m)
            denom = jnp.sum(p, axis=-1, keepdims=True) + jnp.exp(sink - m)
            probs = (p / denom).astype(BF16)
            o = _dot(probs, v_win)
            for i in range(Q_PER_KV):
                heads_out.append(o[i * BLOCK:(i + 1) * BLOCK, :])
        for c in range(ATTN_WIDTH // LANES):
            col = jnp.where(low_half, heads_out[2 * c], heads_out[2 * c + 1])
            mix_ref[rows, c * LANES:(c + 1) * LANES] = col.astype(BF16)

    u = proj[:, ATTN_WIDTH + 2 * KV_WIDTH:]
    u_ext = jnp.concatenate([uprev_ref[...], u], axis=0)
    uprev_ref[...] = u[T - POOL_HALO:, :]
    tpos = s * T + lax.broadcasted_iota(jnp.int32, (T, 1), 0)
    pscale = pscale_ref[...]
    for g, w in enumerate(POOL_WINDOWS):
        cols = slice(g * POOL_GROUP_DIM, (g + 1) * POOL_GROUP_DIM)
        acc = u_ext[:, cols]
        step = 1
        while step < w:
            acc = acc + pltpu.roll(acc, step, 0)
            step *= 2
        wsum = acc[POOL_HALO:, :]
        cnt = jnp.minimum(tpos + 1, w).astype(F32)
        d = (wsum / cnt - u[:, cols]).astype(BF16)
        y = _dot(d, wpool_ref[g]) * pscale[:, cols]
        mix_ref[:, ATTN_WIDTH + g * POOL_GROUP_DIM:ATTN_WIDTH + (g + 1) * POOL_GROUP_DIM] = y.astype(BF16)

    x1 = x + _dot(mix_ref[...], wout_ref[...])

    h2 = _rmsnorm(x1, g2_ref[...]).astype(BF16)
    acc = jnp.zeros((T, D_MODEL), F32)
    for c in range(D_FF // FF_CHUNK):
        up = _dot(h2, wup_ref[:, c * FF_CHUNK:(c + 1) * FF_CHUNK])
        a = jnp.square(jnp.maximum(up, 0.0)).astype(BF16)
        acc = acc + _dot(a, wdown_ref[c * FF_CHUNK:(c + 1) * FF_CHUNK, :])
    x2 = x1 + acc
    o_ref[...] = _rmsnorm(x2, gf_ref[...])


def _rope_tables(seq):
    half = HEAD_DIM // 2
    inv_freq = ROPE_THETA ** (-jnp.arange(half, dtype=F32) / half)
    ang = jnp.arange(seq, dtype=F32)[:, None] * inv_freq[None, :]
    cos, sin = jnp.cos(ang), jnp.sin(ang)
    reps = LANES // HEAD_DIM
    cos_t = jnp.tile(jnp.concatenate([cos, cos], axis=1), (1, reps))
    sin_t = jnp.tile(jnp.concatenate([-sin, sin], axis=1), (1, reps))
    return cos_t, sin_t


def _resident(shape):
    return pl.BlockSpec(shape, lambda b, s: (0,) * len(shape), pipeline_mode=pl.Buffered(1))


@jax.jit
def kernel(x, attn_norm_g, w_in, attn_sinks, w_pool, pool_scale, w_out, mlp_norm_g, w_up,
           w_down, final_norm_g):
    B, S, D = x.shape
    assert D == D_MODEL and S % SEQ_TILE == 0 and attn_norm_g.shape[0] == 1
    T = SEQ_TILE
    cos_t, sin_t = _rope_tables(S)
    row = lambda v: v.reshape(1, -1).astype(F32)

    grid_spec = pl.GridSpec(
        grid=(B, S // T),
        in_specs=[
            pl.BlockSpec(memory_space=pltpu.SMEM),
            pl.BlockSpec((None, T, D), lambda b, s: (b, s, 0)),
            pl.BlockSpec((T, LANES), lambda b, s: (s, 0)),
            pl.BlockSpec((T, LANES), lambda b, s: (s, 0)),
            _resident((1, D)),
            _resident((D, IN_WIDTH)),
            _resident((N_POOL_GROUPS, POOL_GROUP_DIM, POOL_GROUP_DIM)),
            _resident((1, POOL_WIDTH)),
            _resident((D, D)),
            _resident((1, D)),
            _resident((D, D_FF)),
            _resident((D_FF, D)),
            _resident((1, D)),
        ],
        out_specs=pl.BlockSpec((None, T, D), lambda b, s: (b, s, 0)),
        scratch_shapes=[
            pltpu.VMEM((N_KV_HEADS, BLOCK, LANES), BF16),
            pltpu.VMEM((N_KV_HEADS, BLOCK, LANES), BF16),
            pltpu.VMEM((POOL_HALO, POOL_WIDTH), F32),
            pltpu.VMEM((T, D), BF16),
        ],
    )
    return pl.pallas_call(
        _block_kernel,
        grid_spec=grid_spec,
        out_shape=jax.ShapeDtypeStruct((B, S, D), x.dtype),
        compiler_params=pltpu.CompilerParams(
            dimension_semantics=("arbitrary", "arbitrary"),
            vmem_limit_bytes=VMEM_LIMIT_BYTES,
        ),
        name="hymba_block",
    )(attn_sinks[0].astype(F32), x, cos_t, sin_t, row(attn_norm_g[0]), w_in[0].astype(BF16),
      w_pool[0].astype(BF16), row(pool_scale[0]), w_out[0].astype(BF16), row(mlp_norm_g[0]),
      w_up[0].astype(BF16), w_down[0].astype(BF16), row(final_norm_g))
```

```python
import functools

import jax
import jax.numpy as jnp
from jax import lax
from jax.experimental import pallas as pl
from jax.experimental.pallas import tpu as pltpu

D_MODEL = 1024
HEAD_DIM = 64
N_Q_HEADS = 8
N_KV_HEADS = 2
Q_PER_KV = N_Q_HEADS // N_KV_HEADS
ATTN_WIDTH = N_Q_HEADS * HEAD_DIM
KV_WIDTH = N_KV_HEADS * HEAD_DIM
WINDOW = 128
BLOCK = WINDOW
ROPE_THETA = 10000.0
POOL_WINDOWS = (2, 4, 8, 16)
N_POOL_GROUPS = len(POOL_WINDOWS)
POOL_WIDTH = D_MODEL - ATTN_WIDTH
POOL_GROUP_DIM = POOL_WIDTH // N_POOL_GROUPS
POOL_HALO = 16
IN_WIDTH = ATTN_WIDTH + 2 * KV_WIDTH + POOL_WIDTH
D_FF = 4 * D_MODEL
EPS = 1e-6

LANES = 128
SEQ_TILE = 512
FF_CHUNK = 1024
VMEM_LIMIT_BYTES = 56 * 1024 * 1024

F32 = jnp.float32
BF16 = jnp.bfloat16


def _rmsnorm(x, g):
    ms = jnp.mean(x * x, axis=-1, keepdims=True)
    return x * lax.rsqrt(ms + EPS) * g


def _dot(a, b):
    return jnp.dot(a, b, preferred_element_type=F32)


def _rope(xc, cos, sin_signed, first_half):
    swapped = jnp.where(first_half, pltpu.roll(xc, LANES - HEAD_DIM // 2, 1),
                        pltpu.roll(xc, HEAD_DIM // 2, 1))
    return xc * cos + swapped * sin_signed


def _dup_heads(t, low_half):
    r = pltpu.roll(t, HEAD_DIM, 1)
    return jnp.where(low_half, t, r), jnp.where(low_half, r, t)


def _block_kernel(sink_ref, x_ref, cos_ref, sin_ref, g1_ref, win_ref, wpool_ref, pscale_ref,
                  wout_ref, g2_ref, wup_ref, wdown_ref, gf_ref, o_ref,
                  kprev_ref, vprev_ref, uprev_ref, mix_ref):
    T = SEQ_TILE
    s = pl.program_id(1)

    @pl.when(s == 0)
    def _():
        kprev_ref[...] = jnp.zeros_like(kprev_ref)
        vprev_ref[...] = jnp.zeros_like(vprev_ref)
        uprev_ref[...] = jnp.zeros_like(uprev_ref)

    x = x_ref[...]
    h = _rmsnorm(x, g1_ref[...]).astype(BF16)
    proj = _dot(h, win_ref[...])

    lane = lax.broadcasted_iota(jnp.int32, (1, LANES), 1)
    low_half = lane < HEAD_DIM
    first_half = (lane % HEAD_DIM) < (HEAD_DIM // 2)
    cos = cos_ref[...]
    sin_signed = sin_ref[...]

    scale = HEAD_DIM ** -0.5
    q_heads = []
    for c in range(ATTN_WIDTH // LANES):
        qc = _rope(proj[:, c * LANES:(c + 1) * LANES], cos, sin_signed, first_half) * scale
        q_heads.append(jnp.where(low_half, qc, 0.0).astype(BF16))
        q_heads.append(jnp.where(low_half, 0.0, qc).astype(BF16))
    k_r = _rope(proj[:, ATTN_WIDTH:ATTN_WIDTH + KV_WIDTH], cos, sin_signed, first_half)
    k_dup = [t.astype(BF16) for t in _dup_heads(k_r, low_half)]
    k_ext = [jnp.concatenate([kprev_ref[g], k_dup[g]], axis=0) for g in range(N_KV_HEADS)]
    v_t = proj[:, ATTN_WIDTH + KV_WIDTH:ATTN_WIDTH + 2 * KV_WIDTH].T.astype(BF16)
    vt_ext = jnp.concatenate([vprev_ref[...], v_t], axis=1)
    for g in range(N_KV_HEADS):
        kprev_ref[g] = k_dup[g][T - BLOCK:, :]
    vprev_ref[...] = v_t[:, T - BLOCK:]

    kj = lax.broadcasted_iota(jnp.int32, (2 * BLOCK, BLOCK), 0)
    qi = lax.broadcasted_iota(jnp.int32, (2 * BLOCK, BLOCK), 1)
    rel = BLOCK + qi - kj
    band = (rel >= 0) & (rel < WINDOW)
    band_first = band & ((kj >= BLOCK) | (s > 0))

    chains = [(j, g) for j in range(T // BLOCK) for g in range(N_KV_HEADS)]
    logits, probs = {}, {}
    for j, g in chains:
        rows = slice(j * BLOCK, (j + 1) * BLOCK)
        q_stack = jnp.concatenate([q_heads[g * Q_PER_KV + i][rows, :] for i in range(Q_PER_KV)], axis=0)
        k_win = k_ext[g][j * BLOCK:(j + 2) * BLOCK, :]
        logits[j, g] = lax.dot_general(k_win, q_stack, (((1,), (1,)), ((), ())),
                                       preferred_element_type=F32)
    for j, g in chains:
        mask1 = band_first if j == 0 else band
        mask = jnp.concatenate([mask1] * Q_PER_KV, axis=1)
        sink = jnp.concatenate([jnp.full((1, BLOCK), sink_ref[g * Q_PER_KV + i], F32)
                                for i in range(Q_PER_KV)], axis=1)
        lg = jnp.where(mask, logits[j, g], -jnp.inf)
        m = jnp.maximum(jnp.max(lg, axis=0, keepdims=True), sink)
        p = jnp.exp(lg - m)
        denom = jnp.sum(p, axis=0, keepdims=True) + jnp.exp(sink - m)
        probs[j, g] = (p * (1.0 / denom)).astype(BF16)
    for j, g in chains:
        rows = slice(j * BLOCK, (j + 1) * BLOCK)
        vt_win = vt_ext[g * HEAD_DIM:(g + 1) * HEAD_DIM, j * BLOCK:(j + 2) * BLOCK]
        o_t = _dot(vt_win, probs[j, g])
        for cc in range(Q_PER_KV // 2):
            pair = jnp.concatenate([o_t[:, (2 * cc) * BLOCK:(2 * cc + 1) * BLOCK],
                                    o_t[:, (2 * cc + 1) * BLOCK:(2 * cc + 2) * BLOCK]], axis=0)
            c = g * (Q_PER_KV // 2) + cc
            mix_ref[rows, c * LANES:(c + 1) * LANES] = pair.T.astype(BF16)

    u = proj[:, ATTN_WIDTH + 2 * KV_WIDTH:]
    u_ext = jnp.concatenate([uprev_ref[...], u], axis=0)
    uprev_ref[...] = u[T - POOL_HALO:, :]
    tpos = s * T + lax.broadcasted_iota(jnp.int32, (T, 1), 0)
    pscale = pscale_ref[...]
    for g, w in enumerate(POOL_WINDOWS):
        cols = slice(g * POOL_GROUP_DIM, (g + 1) * POOL_GROUP_DIM)
        acc = u_ext[:, cols]
        step = 1
        while step < w:
            acc = acc + pltpu.roll(acc, step, 0)
            step *= 2
        wsum = acc[POOL_HALO:, :]
        cnt = jnp.minimum(tpos + 1, w).astype(F32)
        d = (wsum / cnt - u[:, cols]).astype(BF16)
        y = _dot(d, wpool_ref[g]) * pscale[:, cols]
        mix_ref[:, ATTN_WIDTH + g * POOL_GROUP_DIM:ATTN_WIDTH + (g + 1) * POOL_GROUP_DIM] = y.astype(BF16)

    x1 = x + _dot(mix_ref[...], wout_ref[...])

    h2 = _rmsnorm(x1, g2_ref[...]).astype(BF16)
    acc = jnp.zeros((T, D_MODEL), F32)
    for c in range(D_FF // FF_CHUNK):
        up = _dot(h2, wup_ref[:, c * FF_CHUNK:(c + 1) * FF_CHUNK])
        a = jnp.square(jnp.maximum(up, 0.0)).astype(BF16)
        acc = acc + _dot(a, wdown_ref[c * FF_CHUNK:(c + 1) * FF_CHUNK, :])
    x2 = x1 + acc
    o_ref[...] = _rmsnorm(x2, gf_ref[...])


def _rope_tables(seq):
    half = HEAD_DIM // 2
    inv_freq = ROPE_THETA ** (-jnp.arange(half, dtype=F32) / half)
    ang = jnp.arange(seq, dtype=F32)[:, None] * inv_freq[None, :]
    cos, sin = jnp.cos(ang), jnp.sin(ang)
    reps = LANES // HEAD_DIM
    cos_t = jnp.tile(jnp.concatenate([cos, cos], axis=1), (1, reps))
    sin_t = jnp.tile(jnp.concatenate([-sin, sin], axis=1), (1, reps))
    return cos_t, sin_t


def _resident(shape):
    return pl.BlockSpec(shape, lambda b, s: (0,) * len(shape), pipeline_mode=pl.Buffered(1))


@jax.jit
def kernel(x, attn_norm_g, w_in, attn_sinks, w_pool, pool_scale, w_out, mlp_norm_g, w_up,
           w_down, final_norm_g):
    B, S, D = x.shape
    assert D == D_MODEL and S % SEQ_TILE == 0 and attn_norm_g.shape[0] == 1
    T = SEQ_TILE
    cos_t, sin_t = _rope_tables(S)
    row = lambda v: v.reshape(1, -1).astype(F32)

    grid_spec = pl.GridSpec(
        grid=(B, S // T),
        in_specs=[
            pl.BlockSpec(memory_space=pltpu.SMEM),
            pl.BlockSpec((None, T, D), lambda b, s: (b, s, 0)),
            pl.BlockSpec((T, LANES), lambda b, s: (s, 0)),
            pl.BlockSpec((T, LANES), lambda b, s: (s, 0)),
            _resident((1, D)),
            _resident((D, IN_WIDTH)),
            _resident((N_POOL_GROUPS, POOL_GROUP_DIM, POOL_GROUP_DIM)),
            _resident((1, POOL_WIDTH)),
            _resident((D, D)),
            _resident((1, D)),
            _resident((D, D_FF)),
            _resident((D_FF, D)),
            _resident((1, D)),
        ],
        out_specs=pl.BlockSpec((None, T, D), lambda b, s: (b, s, 0)),
        scratch_shapes=[
            pltpu.VMEM((N_KV_HEADS, BLOCK, LANES), BF16),
            pltpu.VMEM((KV_WIDTH, BLOCK), BF16),
            pltpu.VMEM((POOL_HALO, POOL_WIDTH), F32),
            pltpu.VMEM((T, D), BF16),
        ],
    )
    return pl.pallas_call(
        _block_kernel,
        grid_spec=grid_spec,
        out_shape=jax.ShapeDtypeStruct((B, S, D), x.dtype),
        compiler_params=pltpu.CompilerParams(
            dimension_semantics=("arbitrary", "arbitrary"),
            vmem_limit_bytes=VMEM_LIMIT_BYTES,
        ),
        name="hymba_block",
    )(attn_sinks[0].astype(F32), x, cos_t, sin_t, row(attn_norm_g[0]), w_in[0].astype(BF16),
      w_pool[0].astype(BF16), row(pool_scale[0]), w_out[0].astype(BF16), row(mlp_norm_g[0]),
      w_up[0].astype(BF16), w_down[0].astype(BF16), row(final_norm_g))
```

```python
import functools

import jax
import jax.numpy as jnp
from jax import lax
from jax.experimental import pallas as pl
from jax.experimental.pallas import tpu as pltpu

D_MODEL = 1024
HEAD_DIM = 64
N_Q_HEADS = 8
N_KV_HEADS = 2
Q_PER_KV = N_Q_HEADS // N_KV_HEADS
ATTN_WIDTH = N_Q_HEADS * HEAD_DIM
KV_WIDTH = N_KV_HEADS * HEAD_DIM
WINDOW = 128
BLOCK = WINDOW
ROPE_THETA = 10000.0
POOL_WINDOWS = (2, 4, 8, 16)
N_POOL_GROUPS = len(POOL_WINDOWS)
POOL_WIDTH = D_MODEL - ATTN_WIDTH
POOL_GROUP_DIM = POOL_WIDTH // N_POOL_GROUPS
POOL_HALO = 16
IN_WIDTH = ATTN_WIDTH + 2 * KV_WIDTH + POOL_WIDTH
D_FF = 4 * D_MODEL
EPS = 1e-6

LANES = 128
SEQ_TILE = 512
FF_CHUNK = 1024
VMEM_LIMIT_BYTES = 56 * 1024 * 1024

F32 = jnp.float32
BF16 = jnp.bfloat16


def _rmsnorm(x, g):
    ms = jnp.mean(x * x, axis=-1, keepdims=True)
    return x * lax.rsqrt(ms + EPS) * g


def _dot(a, b):
    return jnp.dot(a, b, preferred_element_type=F32)


def _rope(xc, cos, sin_signed, first_half):
    swapped = jnp.where(first_half, pltpu.roll(xc, LANES - HEAD_DIM // 2, 1),
                        pltpu.roll(xc, HEAD_DIM // 2, 1))
    return xc * cos + swapped * sin_signed


def _dup_heads(t, low_half):
    r = pltpu.roll(t, HEAD_DIM, 1)
    return jnp.where(low_half, t, r), jnp.where(low_half, r, t)


def _token_mixing(s, sink_ref, x_ref, cos_ref, sin_ref, g1_ref, win_ref, wpool_ref, pscale_ref,
                  wout_ref, kprev_ref, vprev_ref, uprev_ref, mix_ref, x1_ref):
    T = SEQ_TILE
    x = x_ref[...]
    h = _rmsnorm(x, g1_ref[...]).astype(BF16)
    proj = _dot(h, win_ref[...])
    yield

    lane = lax.broadcasted_iota(jnp.int32, (1, LANES), 1)
    low_half = lane < HEAD_DIM
    first_half = (lane % HEAD_DIM) < (HEAD_DIM // 2)
    cos = cos_ref[...]
    sin_signed = sin_ref[...]

    scale = HEAD_DIM ** -0.5
    q_heads = []
    for c in range(ATTN_WIDTH // LANES):
        qc = _rope(proj[:, c * LANES:(c + 1) * LANES], cos, sin_signed, first_half) * scale
        q_heads.append(jnp.where(low_half, qc, 0.0).astype(BF16))
        q_heads.append(jnp.where(low_half, 0.0, qc).astype(BF16))
    k_r = _rope(proj[:, ATTN_WIDTH:ATTN_WIDTH + KV_WIDTH], cos, sin_signed, first_half)
    k_dup = [t.astype(BF16) for t in _dup_heads(k_r, low_half)]
    k_ext = [jnp.concatenate([kprev_ref[g], k_dup[g]], axis=0) for g in range(N_KV_HEADS)]
    v_t = proj[:, ATTN_WIDTH + KV_WIDTH:ATTN_WIDTH + 2 * KV_WIDTH].T.astype(BF16)
    vt_ext = jnp.concatenate([vprev_ref[...], v_t], axis=1)
    for g in range(N_KV_HEADS):
        kprev_ref[g] = k_dup[g][T - BLOCK:, :]
    vprev_ref[...] = v_t[:, T - BLOCK:]

    kj = lax.broadcasted_iota(jnp.int32, (2 * BLOCK, BLOCK), 0)
    qi = lax.broadcasted_iota(jnp.int32, (2 * BLOCK, BLOCK), 1)
    rel = BLOCK + qi - kj
    band = (rel >= 0) & (rel < WINDOW)
    band_first = band & ((kj >= BLOCK) | (s > 0))
    yield

    chains = [(j, g) for j in range(T // BLOCK) for g in range(N_KV_HEADS)]
    logits, probs = {}, {}
    for j, g in chains:
        rows = slice(j * BLOCK, (j + 1) * BLOCK)
        q_stack = jnp.concatenate([q_heads[g * Q_PER_KV + i][rows, :] for i in range(Q_PER_KV)], axis=0)
        k_win = k_ext[g][j * BLOCK:(j + 2) * BLOCK, :]
        logits[j, g] = lax.dot_general(k_win, q_stack, (((1,), (1,)), ((), ())),
                                       preferred_element_type=F32)
    yield
    for n, (j, g) in enumerate(chains):
        mask1 = band_first if j == 0 else band
        mask = jnp.concatenate([mask1] * Q_PER_KV, axis=1)
        sink = jnp.concatenate([jnp.full((1, BLOCK), sink_ref[g * Q_PER_KV + i], F32)
                                for i in range(Q_PER_KV)], axis=1)
        lg = jnp.where(mask, logits[j, g], -jnp.inf)
        m = jnp.maximum(jnp.max(lg, axis=0, keepdims=True), sink)
        p = jnp.exp(lg - m)
        denom = jnp.sum(p, axis=0, keepdims=True) + jnp.exp(sink - m)
        probs[j, g] = (p * (1.0 / denom)).astype(BF16)
        if n % 2 == 1:
            yield
    for j, g in chains:
        rows = slice(j * BLOCK, (j + 1) * BLOCK)
        vt_win = vt_ext[g * HEAD_DIM:(g + 1) * HEAD_DIM, j * BLOCK:(j + 2) * BLOCK]
        o_t = _dot(vt_win, probs[j, g])
        for cc in range(Q_PER_KV // 2):
            pair = jnp.concatenate([o_t[:, (2 * cc) * BLOCK:(2 * cc + 1) * BLOCK],
                                    o_t[:, (2 * cc + 1) * BLOCK:(2 * cc + 2) * BLOCK]], axis=0)
            c = g * (Q_PER_KV // 2) + cc
            mix_ref[rows, c * LANES:(c + 1) * LANES] = pair.T.astype(BF16)
    yield

    u = proj[:, ATTN_WIDTH + 2 * KV_WIDTH:]
    u_ext = jnp.concatenate([uprev_ref[...], u], axis=0)
    uprev_ref[...] = u[T - POOL_HALO:, :]
    tpos = s * T + lax.broadcasted_iota(jnp.int32, (T, 1), 0)
    pscale = pscale_ref[...]
    for g, w in enumerate(POOL_WINDOWS):
        cols = slice(g * POOL_GROUP_DIM, (g + 1) * POOL_GROUP_DIM)
        acc = u_ext[:, cols]
        step = 1
        while step < w:
            acc = acc + pltpu.roll(acc, step, 0)
            step *= 2
        wsum = acc[POOL_HALO:, :]
        cnt = jnp.minimum(tpos + 1, w).astype(F32)
        d = (wsum / cnt - u[:, cols]).astype(BF16)
        y = _dot(d, wpool_ref[g]) * pscale[:, cols]
        mix_ref[:, ATTN_WIDTH + g * POOL_GROUP_DIM:ATTN_WIDTH + (g + 1) * POOL_GROUP_DIM] = y.astype(BF16)
    yield

    x1_ref[...] = x + _dot(mix_ref[...], wout_ref[...])


def _channel_mixing(x1_ref, g2_ref, wup_ref, wdown_ref, gf_ref, o_ref):
    x1 = x1_ref[...]
    h2 = _rmsnorm(x1, g2_ref[...]).astype(BF16)
    yield
    acc = jnp.zeros((SEQ_TILE, D_MODEL), F32)
    for c in range(D_FF // FF_CHUNK):
        up = _dot(h2, wup_ref[:, c * FF_CHUNK:(c + 1) * FF_CHUNK])
        a = jnp.square(jnp.maximum(up, 0.0)).astype(BF16)
        yield
        acc = acc + _dot(a, wdown_ref[c * FF_CHUNK:(c + 1) * FF_CHUNK, :])
        yield
    x2 = x1 + acc
    o_ref[...] = _rmsnorm(x2, gf_ref[...])


def _block_kernel(tiles_per_seq, sink_ref, x_ref, cos_ref, sin_ref, g1_ref, win_ref, wpool_ref,
                  pscale_ref, wout_ref, g2_ref, wup_ref, wdown_ref, gf_ref, o_ref,
                  kprev_ref, vprev_ref, uprev_ref, mix_ref, x1_ref):
    t = pl.program_id(0)
    n_tiles = pl.num_programs(0) - 1
    s = jnp.minimum(t, n_tiles - 1) % tiles_per_seq

    @pl.when(t == 0)
    def _():
        x1_ref[...] = jnp.zeros_like(x1_ref)

    @pl.when(s == 0)
    def _():
        kprev_ref[...] = jnp.zeros_like(kprev_ref)
        vprev_ref[...] = jnp.zeros_like(vprev_ref)
        uprev_ref[...] = jnp.zeros_like(uprev_ref)

    halves = [
        _channel_mixing(x1_ref, g2_ref, wup_ref, wdown_ref, gf_ref, o_ref),
        _token_mixing(s, sink_ref, x_ref, cos_ref, sin_ref, g1_ref, win_ref, wpool_ref, pscale_ref,
                      wout_ref, kprev_ref, vprev_ref, uprev_ref, mix_ref, x1_ref),
    ]
    while halves:
        for gen in list(halves):
            if next(gen, StopIteration) is StopIteration:
                halves.remove(gen)


def _rope_tables(seq):
    half = HEAD_DIM // 2
    inv_freq = ROPE_THETA ** (-jnp.arange(half, dtype=F32) / half)
    ang = jnp.arange(seq, dtype=F32)[:, None] * inv_freq[None, :]
    cos, sin = jnp.cos(ang), jnp.sin(ang)
    reps = LANES // HEAD_DIM
    cos_t = jnp.tile(jnp.concatenate([cos, cos], axis=1), (1, reps))
    sin_t = jnp.tile(jnp.concatenate([-sin, sin], axis=1), (1, reps))
    return cos_t, sin_t


def _resident(shape):
    return pl.BlockSpec(shape, lambda t: (0,) * len(shape), pipeline_mode=pl.Buffered(1))


@jax.jit
def kernel(x, attn_norm_g, w_in, attn_sinks, w_pool, pool_scale, w_out, mlp_norm_g, w_up,
           w_down, final_norm_g):
    B, S, D = x.shape
    assert D == D_MODEL and S % SEQ_TILE == 0 and attn_norm_g.shape[0] == 1
    T = SEQ_TILE
    tiles_per_seq = S // T
    n_tiles = B * tiles_per_seq
    cos_t, sin_t = _rope_tables(S)
    row = lambda v: v.reshape(1, -1).astype(F32)

    def mix_tile(t):
        n = jnp.minimum(t, n_tiles - 1)
        return n // tiles_per_seq, n % tiles_per_seq

    def mlp_tile(t):
        n = jnp.maximum(t - 1, 0)
        return n // tiles_per_seq, n % tiles_per_seq

    grid_spec = pl.GridSpec(
        grid=(n_tiles + 1,),
        in_specs=[
            pl.BlockSpec(memory_space=pltpu.SMEM),
            pl.BlockSpec((None, T, D), lambda t: (*mix_tile(t), 0)),
            pl.BlockSpec((T, LANES), lambda t: (mix_tile(t)[1], 0)),
            pl.BlockSpec((T, LANES), lambda t: (mix_tile(t)[1], 0)),
            _resident((1, D)),
            _resident((D, IN_WIDTH)),
            _resident((N_POOL_GROUPS, POOL_GROUP_DIM, POOL_GROUP_DIM)),
            _resident((1, POOL_WIDTH)),
            _resident((D, D)),
            _resident((1, D)),
            _resident((D, D_FF)),
            _resident((D_FF, D)),
            _resident((1, D)),
        ],
        out_specs=pl.BlockSpec((None, T, D), lambda t: (*mlp_tile(t), 0)),
        scratch_shapes=[
            pltpu.VMEM((N_KV_HEADS, BLOCK, LANES), BF16),
            pltpu.VMEM((KV_WIDTH, BLOCK), BF16),
            pltpu.VMEM((POOL_HALO, POOL_WIDTH), F32),
            pltpu.VMEM((T, D), BF16),
            pltpu.VMEM((T, D), F32),
        ],
    )
    return pl.pallas_call(
        functools.partial(_block_kernel, tiles_per_seq),
        grid_spec=grid_spec,
        out_shape=jax.ShapeDtypeStruct((B, S, D), x.dtype),
        compiler_params=pltpu.CompilerParams(
            dimension_semantics=("arbitrary",),
            vmem_limit_bytes=VMEM_LIMIT_BYTES,
        ),
        name="hymba_block",
    )(attn_sinks[0].astype(F32), x, cos_t, sin_t, row(attn_norm_g[0]), w_in[0].astype(BF16),
      w_pool[0].astype(BF16), row(pool_scale[0]), w_out[0].astype(BF16), row(mlp_norm_g[0]),
      w_up[0].astype(BF16), w_down[0].astype(BF16), row(final_norm_g))
```

```python
import functools

import jax
import jax.numpy as jnp
from jax import lax
from jax.experimental import pallas as pl
from jax.experimental.pallas import tpu as pltpu

D_MODEL = 1024
HEAD_DIM = 64
N_Q_HEADS = 8
N_KV_HEADS = 2
Q_PER_KV = N_Q_HEADS // N_KV_HEADS
ATTN_WIDTH = N_Q_HEADS * HEAD_DIM
KV_WIDTH = N_KV_HEADS * HEAD_DIM
WINDOW = 128
BLOCK = WINDOW
ROPE_THETA = 10000.0
POOL_WINDOWS = (2, 4, 8, 16)
N_POOL_GROUPS = len(POOL_WINDOWS)
POOL_WIDTH = D_MODEL - ATTN_WIDTH
POOL_GROUP_DIM = POOL_WIDTH // N_POOL_GROUPS
POOL_HALO = 16
IN_WIDTH = ATTN_WIDTH + 2 * KV_WIDTH + POOL_WIDTH
D_FF = 4 * D_MODEL
EPS = 1e-6

LANES = 128
SEQ_TILE = 512
FF_CHUNK = 1024
FF_HALF = FF_CHUNK // 2
PIPELINE_LAG = 2
VMEM_LIMIT_BYTES = 56 * 1024 * 1024

F32 = jnp.float32
BF16 = jnp.bfloat16


def _rmsnorm(x, g):
    ms = jnp.mean(x * x, axis=-1, keepdims=True)
    return x * lax.rsqrt(ms + EPS) * g


_dot = functools.partial(jnp.dot, preferred_element_type=F32)


def _rope(xc, cos, sin_signed, first_half):
    swapped = jnp.where(first_half, pltpu.roll(xc, LANES - HEAD_DIM // 2, 1),
                        pltpu.roll(xc, HEAD_DIM // 2, 1))
    return xc * cos + swapped * sin_signed


def _dup_heads(t, low_half):
    r = pltpu.roll(t, HEAD_DIM, 1)
    return jnp.where(low_half, t, r), jnp.where(low_half, r, t)


def _token_mixing(s, sink_ref, x_ref, cos_ref, sin_ref, g1_ref, win_ref, wpool_ref, pscale_ref,
                  wout_ref, g2_ref, wup_ref, kprev_ref, vprev_ref, uprev_ref, mix_ref, x1_ref, h2_ref,
                  act_ref):
    T = SEQ_TILE
    x = x_ref[...]
    h = _rmsnorm(x, g1_ref[...]).astype(BF16)
    yield "norm1"
    proj = _dot(h, win_ref[...])
    yield "in_proj"

    lane = lax.broadcasted_iota(jnp.int32, (1, LANES), 1)
    low_half = lane < HEAD_DIM
    first_half = (lane % HEAD_DIM) < (HEAD_DIM // 2)
    cos = cos_ref[...]
    sin_signed = sin_ref[...]

    scale = HEAD_DIM ** -0.5
    q_heads = []
    for c in range(ATTN_WIDTH // LANES):
        qc = _rope(proj[:, c * LANES:(c + 1) * LANES], cos, sin_signed, first_half) * scale
        q_heads.append(jnp.where(low_half, qc, 0.0).astype(BF16))
        q_heads.append(jnp.where(low_half, 0.0, qc).astype(BF16))
    k_r = _rope(proj[:, ATTN_WIDTH:ATTN_WIDTH + KV_WIDTH], cos, sin_signed, first_half)
    k_dup = [t.astype(BF16) for t in _dup_heads(k_r, low_half)]
    k_ext = [jnp.concatenate([kprev_ref[g], k_dup[g]], axis=0) for g in range(N_KV_HEADS)]
    v_t = proj[:, ATTN_WIDTH + KV_WIDTH:ATTN_WIDTH + 2 * KV_WIDTH].T.astype(BF16)
    vt_ext = jnp.concatenate([vprev_ref[...], v_t], axis=1)
    for g in range(N_KV_HEADS):
        kprev_ref[g] = k_dup[g][T - BLOCK:, :]
    vprev_ref[...] = v_t[:, T - BLOCK:]

    kj = lax.broadcasted_iota(jnp.int32, (2 * BLOCK, BLOCK), 0)
    qi = lax.broadcasted_iota(jnp.int32, (2 * BLOCK, BLOCK), 1)
    rel = BLOCK + qi - kj
    band = (rel >= 0) & (rel < WINDOW)
    band_first = band & ((kj >= BLOCK) | (s > 0))
    yield "rope"

    chains = [(j, g) for j in range(T // BLOCK) for g in range(N_KV_HEADS)]
    logits = {}
    for j, g in chains:
        rows = slice(j * BLOCK, (j + 1) * BLOCK)
        q_stack = jnp.concatenate([q_heads[g * Q_PER_KV + i][rows, :] for i in range(Q_PER_KV)], axis=0)
        k_win = k_ext[g][j * BLOCK:(j + 2) * BLOCK, :]
        logits[j, g] = lax.dot_general(k_win, q_stack, (((1,), (1,)), ((), ())),
                                       preferred_element_type=F32)
    yield "qk"
    for j in range(T // BLOCK):
        probs = {}
        for g in range(N_KV_HEADS):
            mask1 = band_first if j == 0 else band
            mask = jnp.concatenate([mask1] * Q_PER_KV, axis=1)
            sink = jnp.concatenate([jnp.full((1, BLOCK), sink_ref[g * Q_PER_KV + i], F32)
                                    for i in range(Q_PER_KV)], axis=1)
            lg = jnp.where(mask, logits[j, g], -jnp.inf)
            m = jnp.maximum(jnp.max(lg, axis=0, keepdims=True), sink)
            p = jnp.exp(lg - m)
            denom = jnp.sum(p, axis=0, keepdims=True) + jnp.exp(sink - m)
            probs[g] = (p * (1.0 / denom)).astype(BF16)
        yield "softmax"
        rows = slice(j * BLOCK, (j + 1) * BLOCK)
        for g in range(N_KV_HEADS):
            vt_win = vt_ext[g * HEAD_DIM:(g + 1) * HEAD_DIM, j * BLOCK:(j + 2) * BLOCK]
            o_t = _dot(vt_win, probs[g])
            for cc in range(Q_PER_KV // 2):
                pair = jnp.concatenate([o_t[:, (2 * cc) * BLOCK:(2 * cc + 1) * BLOCK],
                                        o_t[:, (2 * cc + 1) * BLOCK:(2 * cc + 2) * BLOCK]], axis=0)
                c = g * (Q_PER_KV // 2) + cc
                mix_ref[rows, c * LANES:(c + 1) * LANES] = pair.T.astype(BF16)
        yield "pv"

    u = proj[:, ATTN_WIDTH + 2 * KV_WIDTH:]
    u_ext = jnp.concatenate([uprev_ref[...], u], axis=0)
    uprev_ref[...] = u[T - POOL_HALO:, :]
    tpos = s * T + lax.broadcasted_iota(jnp.int32, (T, 1), 0)
    pscale = pscale_ref[...]
    pooled = []
    for g, w in enumerate(POOL_WINDOWS):
        cols = slice(g * POOL_GROUP_DIM, (g + 1) * POOL_GROUP_DIM)
        acc = u_ext[:, cols]
        step = 1
        while step < w:
            acc = acc + pltpu.roll(acc, step, 0)
            step *= 2
        wsum = acc[POOL_HALO:, :]
        cnt = jnp.minimum(tpos + 1, w).astype(F32)
        pooled.append((wsum / cnt - u[:, cols]).astype(BF16))
    yield "pool_means"
    for g in range(N_POOL_GROUPS):
        cols = slice(g * POOL_GROUP_DIM, (g + 1) * POOL_GROUP_DIM)
        y = _dot(pooled[g], wpool_ref[g]) * pscale[:, cols]
        mix_ref[:, ATTN_WIDTH + g * POOL_GROUP_DIM:ATTN_WIDTH + (g + 1) * POOL_GROUP_DIM] = y.astype(BF16)
    yield "pool_proj"

    x1 = x + _dot(mix_ref[...], wout_ref[...])
    x1_ref[...] = x1
    yield "out_proj"
    h2 = _rmsnorm(x1, g2_ref[...]).astype(BF16)
    h2_ref[...] = h2
    yield "norm2"
    for i in range(2):
        lo = i * FF_HALF
        act_ref[:, lo:lo + FF_HALF] = _mlp_act(_dot(h2, wup_ref[:, lo:lo + FF_HALF]))
        yield "up_first"


def _mlp_act(up):
    return jnp.square(jnp.maximum(up, 0.0)).astype(BF16)


def _channel_mixing(x1_ref, h2_ref, act_ref, wup_ref, wdown_ref, x2_ref):
    x2_ref[...] = x1_ref[...]
    yield "load_x1"
    for c in range(D_FF // FF_CHUNK):
        if c == 0:
            a = act_ref[...]
        else:
            acts = []
            for lo in (c * FF_CHUNK, c * FF_CHUNK + FF_HALF):
                acts.append(_mlp_act(_dot(h2_ref[...], wup_ref[:, lo:lo + FF_HALF])))
                yield "up"
            a = jnp.concatenate(acts, axis=1)
        x2_ref[...] += _dot(a, wdown_ref[c * FF_CHUNK:(c + 1) * FF_CHUNK, :])
        yield "down"


_A_STAGES = (["norm1", "in_proj", "rope", "qk"] + ["softmax", "pv"] * (SEQ_TILE // BLOCK)
             + ["pool_means", "pool_proj", "out_proj", "norm2", "up_first", "up_first"])
_B_STAGES = ["load_x1", "down"] + ["up", "up", "down"] * (D_FF // FF_CHUNK - 1)
TRACE_ORDER = "A BB AA B AA B AA B AA B AA B AA B A BB A B AA A"


def _block_kernel(tiles_per_seq, sink_ref, x_ref, cos_ref, sin_ref, g1_ref, win_ref, wpool_ref,
                  pscale_ref, wout_ref, g2_ref, wup_ref, wdown_ref, gf_ref, o_ref,
                  kprev_ref, vprev_ref, uprev_ref, mix_ref, x1_ref, h2_ref, act_ref, x2_ref):
    t = pl.program_id(0)
    n_tiles = pl.num_programs(0) - PIPELINE_LAG
    s = jnp.minimum(t, n_tiles - 1) % tiles_per_seq

    @pl.when(t == 0)
    def _():
        x1_ref[...] = jnp.zeros_like(x1_ref)
        h2_ref[...] = jnp.zeros_like(h2_ref)
        act_ref[...] = jnp.zeros_like(act_ref)
        x2_ref[...] = jnp.zeros_like(x2_ref)

    @pl.when(s == 0)
    def _():
        kprev_ref[...] = jnp.zeros_like(kprev_ref)
        vprev_ref[...] = jnp.zeros_like(vprev_ref)
        uprev_ref[...] = jnp.zeros_like(uprev_ref)

    o_ref[...] = _rmsnorm(x2_ref[...], gf_ref[...])

    gens = {
        "A": _token_mixing(s, sink_ref, x_ref, cos_ref, sin_ref, g1_ref, win_ref, wpool_ref, pscale_ref,
                           wout_ref, g2_ref, wup_ref, kprev_ref, vprev_ref, uprev_ref, mix_ref, x1_ref,
                           h2_ref, act_ref),
        "B": _channel_mixing(x1_ref, h2_ref, act_ref, wup_ref, wdown_ref, x2_ref),
    }
    expected = {"A": list(_A_STAGES), "B": list(_B_STAGES)}
    traced = []
    for who in TRACE_ORDER.replace(" ", ""):
        stage = next(gens[who])
        assert stage == expected[who].pop(0), (who, stage)
        traced.append(who + ":" + stage)
    assert not expected["A"] and not expected["B"], expected
    assert traced.index("B:load_x1") < traced.index("A:out_proj")
    assert max(i for i, v in enumerate(traced) if v == "B:up") < traced.index("A:norm2")
    assert traced.index("B:down") < traced.index("A:up_first")


def _rope_tables(seq):
    half = HEAD_DIM // 2
    inv_freq = ROPE_THETA ** (-jnp.arange(half, dtype=F32) / half)
    ang = jnp.arange(seq, dtype=F32)[:, None] * inv_freq[None, :]
    cos, sin = jnp.cos(ang), jnp.sin(ang)
    reps = LANES // HEAD_DIM
    cos_t = jnp.tile(jnp.concatenate([cos, cos], axis=1), (1, reps))
    sin_t = jnp.tile(jnp.concatenate([-sin, sin], axis=1), (1, reps))
    return cos_t, sin_t


def _resident(shape):
    return pl.BlockSpec(shape, lambda t: (0,) * len(shape), pipeline_mode=pl.Buffered(1))


@jax.jit
def kernel(x, attn_norm_g, w_in, attn_sinks, w_pool, pool_scale, w_out, mlp_norm_g, w_up,
           w_down, final_norm_g):
    B, S, D = x.shape
    assert D == D_MODEL and S % SEQ_TILE == 0 and attn_norm_g.shape[0] == 1
    T = SEQ_TILE
    tiles_per_seq = S // T
    n_tiles = B * tiles_per_seq
    cos_t, sin_t = _rope_tables(S)
    row = lambda v: v.reshape(1, -1).astype(F32)

    def mix_tile(t):
        n = jnp.minimum(t, n_tiles - 1)
        return n // tiles_per_seq, n % tiles_per_seq

    def out_tile(t):
        n = jnp.maximum(t - PIPELINE_LAG, 0)
        return n // tiles_per_seq, n % tiles_per_seq

    grid_spec = pl.GridSpec(
        grid=(n_tiles + PIPELINE_LAG,),
        in_specs=[
            pl.BlockSpec(memory_space=pltpu.SMEM),
            pl.BlockSpec((None, T, D), lambda t: (*mix_tile(t), 0)),
            pl.BlockSpec((T, LANES), lambda t: (mix_tile(t)[1], 0)),
            pl.BlockSpec((T, LANES), lambda t: (mix_tile(t)[1], 0)),
            _resident((1, D)),
            _resident((D, IN_WIDTH)),
            _resident((N_POOL_GROUPS, POOL_GROUP_DIM, POOL_GROUP_DIM)),
            _resident((1, POOL_WIDTH)),
            _resident((D, D)),
            _resident((1, D)),
            _resident((D, D_FF)),
            _resident((D_FF, D)),
            _resident((1, D)),
        ],
        out_specs=pl.BlockSpec((None, T, D), lambda t: (*out_tile(t), 0)),
        scratch_shapes=[
            pltpu.VMEM((N_KV_HEADS, BLOCK, LANES), BF16),
            pltpu.VMEM((KV_WIDTH, BLOCK), BF16),
            pltpu.VMEM((POOL_HALO, POOL_WIDTH), F32),
            pltpu.VMEM((T, D), BF16),
            pltpu.VMEM((T, D), F32),
            pltpu.VMEM((T, D), BF16),
            pltpu.VMEM((T, FF_CHUNK), BF16),
            pltpu.VMEM((T, D), F32),
        ],
    )
    return pl.pallas_call(
        functools.partial(_block_kernel, tiles_per_seq),
        grid_spec=grid_spec,
        out_shape=jax.ShapeDtypeStruct((B, S, D), x.dtype),
        compiler_params=pltpu.CompilerParams(
            dimension_semantics=("arbitrary",),
            vmem_limit_bytes=VMEM_LIMIT_BYTES,
        ),
        name="hymba_block",
    )(attn_sinks[0].astype(F32), x, cos_t, sin_t, row(attn_norm_g[0]), w_in[0].astype(BF16),
      w_pool[0].astype(BF16), row(pool_scale[0]), w_out[0].astype(BF16), row(mlp_norm_g[0]),
      w_up[0].astype(BF16), w_down[0].astype(BF16), row(final_norm_g))
```

```python
import functools

import jax
import jax.numpy as jnp
from jax import lax
from jax.experimental import pallas as pl
from jax.experimental.pallas import tpu as pltpu

D_MODEL = 1024
HEAD_DIM = 64
N_Q_HEADS = 8
N_KV_HEADS = 2
Q_PER_KV = N_Q_HEADS // N_KV_HEADS
ATTN_WIDTH = N_Q_HEADS * HEAD_DIM
KV_WIDTH = N_KV_HEADS * HEAD_DIM
WINDOW = 128
BLOCK = WINDOW
ROPE_THETA = 10000.0
POOL_WINDOWS = (2, 4, 8, 16)
N_POOL_GROUPS = len(POOL_WINDOWS)
POOL_WIDTH = D_MODEL - ATTN_WIDTH
POOL_GROUP_DIM = POOL_WIDTH // N_POOL_GROUPS
POOL_HALO = 16
IN_WIDTH = ATTN_WIDTH + 2 * KV_WIDTH + POOL_WIDTH
D_FF = 4 * D_MODEL
EPS = 1e-6

LANES = 128
SEQ_TILE = 512
FF_CHUNK = 1024
VMEM_LIMIT_BYTES = 56 * 1024 * 1024

F32 = jnp.float32
BF16 = jnp.bfloat16


def _rmsnorm(x, g):
    ms = jnp.mean(x * x, axis=-1, keepdims=True)
    return x * lax.rsqrt(ms + EPS) * g


def _dot(a, b):
    return jnp.dot(a, b, preferred_element_type=F32)


def _rope(xc, cos, sin_signed, first_half):
    swapped = jnp.where(first_half, pltpu.roll(xc, LANES - HEAD_DIM // 2, 1),
                        pltpu.roll(xc, HEAD_DIM // 2, 1))
    return xc * cos + swapped * sin_signed


def _dup_heads(t, low_half):
    r = pltpu.roll(t, HEAD_DIM, 1)
    return jnp.where(low_half, t, r), jnp.where(low_half, r, t)


def _token_mixing(s, sink_ref, x_ref, cos_ref, sin_ref, g1_ref, win_ref, woutc_ref,
                  kprev_ref, vprev_ref, uprev_ref, mix_ref, x1_ref):
    T = SEQ_TILE
    x = x_ref[...]
    h = _rmsnorm(x, g1_ref[...]).astype(BF16)
    proj = _dot(h, win_ref[...])
    yield

    lane = lax.broadcasted_iota(jnp.int32, (1, LANES), 1)
    low_half = lane < HEAD_DIM
    first_half = (lane % HEAD_DIM) < (HEAD_DIM // 2)
    cos = cos_ref[...]
    sin_signed = sin_ref[...]

    scale = HEAD_DIM ** -0.5
    q_heads = []
    for c in range(ATTN_WIDTH // LANES):
        qc = _rope(proj[:, c * LANES:(c + 1) * LANES], cos, sin_signed, first_half) * scale
        q_heads.append(jnp.where(low_half, qc, 0.0).astype(BF16))
        q_heads.append(jnp.where(low_half, 0.0, qc).astype(BF16))
    k_r = _rope(proj[:, ATTN_WIDTH:ATTN_WIDTH + KV_WIDTH], cos, sin_signed, first_half)
    k_dup = [t.astype(BF16) for t in _dup_heads(k_r, low_half)]
    k_ext = [jnp.concatenate([kprev_ref[g], k_dup[g]], axis=0) for g in range(N_KV_HEADS)]
    v_t = proj[:, ATTN_WIDTH + KV_WIDTH:ATTN_WIDTH + 2 * KV_WIDTH].T.astype(BF16)
    vt_ext = jnp.concatenate([vprev_ref[...], v_t], axis=1)
    for g in range(N_KV_HEADS):
        kprev_ref[g] = k_dup[g][T - BLOCK:, :]
    vprev_ref[...] = v_t[:, T - BLOCK:]

    kj = lax.broadcasted_iota(jnp.int32, (2 * BLOCK, BLOCK), 0)
    qi = lax.broadcasted_iota(jnp.int32, (2 * BLOCK, BLOCK), 1)
    rel = BLOCK + qi - kj
    band = (rel >= 0) & (rel < WINDOW)
    band_first = band & ((kj >= BLOCK) | (s > 0))
    yield

    chains = [(j, g) for j in range(T // BLOCK) for g in range(N_KV_HEADS)]
    logits, probs = {}, {}
    for j, g in chains:
        rows = slice(j * BLOCK, (j + 1) * BLOCK)
        q_stack = jnp.concatenate([q_heads[g * Q_PER_KV + i][rows, :] for i in range(Q_PER_KV)], axis=0)
        k_win = k_ext[g][j * BLOCK:(j + 2) * BLOCK, :]
        logits[j, g] = lax.dot_general(k_win, q_stack, (((1,), (1,)), ((), ())),
                                       preferred_element_type=F32)
    yield
    for n, (j, g) in enumerate(chains):
        mask1 = band_first if j == 0 else band
        mask = jnp.concatenate([mask1] * Q_PER_KV, axis=1)
        sink = jnp.concatenate([jnp.full((1, BLOCK), sink_ref[g * Q_PER_KV + i], F32)
                                for i in range(Q_PER_KV)], axis=1)
        lg = jnp.where(mask, logits[j, g], -jnp.inf)
        m = jnp.maximum(jnp.max(lg, axis=0, keepdims=True), sink)
        p = jnp.exp(lg - m)
        denom = jnp.sum(p, axis=0, keepdims=True) + jnp.exp(sink - m)
        probs[j, g] = (p * (1.0 / denom)).astype(BF16)
        if n % 2 == 1:
            yield
    for j, g in chains:
        rows = slice(j * BLOCK, (j + 1) * BLOCK)
        vt_win = vt_ext[g * HEAD_DIM:(g + 1) * HEAD_DIM, j * BLOCK:(j + 2) * BLOCK]
        o_t = _dot(vt_win, probs[j, g])
        for cc in range(Q_PER_KV // 2):
            pair = jnp.concatenate([o_t[:, (2 * cc) * BLOCK:(2 * cc + 1) * BLOCK],
                                    o_t[:, (2 * cc + 1) * BLOCK:(2 * cc + 2) * BLOCK]], axis=0)
            c = g * (Q_PER_KV // 2) + cc
            mix_ref[rows, c * LANES:(c + 1) * LANES] = pair.T.astype(BF16)
    yield

    u = proj[:, ATTN_WIDTH + 2 * KV_WIDTH:]
    u_ext = jnp.concatenate([uprev_ref[...], u], axis=0)
    uprev_ref[...] = u[T - POOL_HALO:, :]
    tpos = s * T + lax.broadcasted_iota(jnp.int32, (T, 1), 0)
    for g, w in enumerate(POOL_WINDOWS):
        cols = slice(g * POOL_GROUP_DIM, (g + 1) * POOL_GROUP_DIM)
        acc = u_ext[:, cols]
        step = 1
        while step < w:
            acc = acc + pltpu.roll(acc, step, 0)
            step *= 2
        wsum = acc[POOL_HALO:, :]
        cnt = jnp.minimum(tpos + 1, w).astype(F32)
        d = (wsum / cnt - u[:, cols]).astype(BF16)
        mix_ref[:, ATTN_WIDTH + g * POOL_GROUP_DIM:ATTN_WIDTH + (g + 1) * POOL_GROUP_DIM] = d
    yield

    x1_ref[...] = x + _dot(mix_ref[...], woutc_ref[...])


def _fold_out_projection(wpool_ref, pscale_ref, wout_ref, woutc_ref):
    woutc_ref[0:ATTN_WIDTH, :] = wout_ref[0:ATTN_WIDTH, :].astype(BF16)
    for g in range(N_POOL_GROUPS):
        cols = slice(g * POOL_GROUP_DIM, (g + 1) * POOL_GROUP_DIM)
        rows = slice(ATTN_WIDTH + g * POOL_GROUP_DIM, ATTN_WIDTH + (g + 1) * POOL_GROUP_DIM)
        scaled = wpool_ref[g] * pscale_ref[:, cols]
        folded = jnp.dot(scaled, wout_ref[rows, :], precision=lax.Precision.HIGHEST,
                         preferred_element_type=F32)
        woutc_ref[rows, :] = folded.astype(BF16)


def _channel_mixing(x1_ref, g2_ref, wup_ref, wdown_ref, gf_ref, o_ref):
    x1 = x1_ref[...]
    h2 = _rmsnorm(x1, g2_ref[...]).astype(BF16)
    yield
    acc = jnp.zeros((SEQ_TILE, D_MODEL), F32)
    for c in range(D_FF // FF_CHUNK):
        up = _dot(h2, wup_ref[:, c * FF_CHUNK:(c + 1) * FF_CHUNK])
        a = jnp.square(jnp.maximum(up, 0.0)).astype(BF16)
        yield
        acc = acc + _dot(a, wdown_ref[c * FF_CHUNK:(c + 1) * FF_CHUNK, :])
        yield
    x2 = x1 + acc
    o_ref[...] = _rmsnorm(x2, gf_ref[...])


def _block_kernel(tiles_per_seq, sink_ref, x_ref, cos_ref, sin_ref, g1_ref, win_ref, wpool_ref,
                  pscale_ref, wout_ref, g2_ref, wup_ref, wdown_ref, gf_ref, o_ref,
                  kprev_ref, vprev_ref, uprev_ref, mix_ref, x1_ref, woutc_ref):
    t = pl.program_id(0)
    n_tiles = pl.num_programs(0) - 1
    s = jnp.minimum(t, n_tiles - 1) % tiles_per_seq

    @pl.when(t == 0)
    def _():
        x1_ref[...] = jnp.zeros_like(x1_ref)
        _fold_out_projection(wpool_ref, pscale_ref, wout_ref, woutc_ref)

    @pl.when(s == 0)
    def _():
        kprev_ref[...] = jnp.zeros_like(kprev_ref)
        vprev_ref[...] = jnp.zeros_like(vprev_ref)
        uprev_ref[...] = jnp.zeros_like(uprev_ref)

    halves = [
        _channel_mixing(x1_ref, g2_ref, wup_ref, wdown_ref, gf_ref, o_ref),
        _token_mixing(s, sink_ref, x_ref, cos_ref, sin_ref, g1_ref, win_ref, woutc_ref,
                      kprev_ref, vprev_ref, uprev_ref, mix_ref, x1_ref),
    ]
    while halves:
        for gen in list(halves):
            if next(gen, StopIteration) is StopIteration:
                halves.remove(gen)


def _rope_tables(seq):
    half = HEAD_DIM // 2
    inv_freq = ROPE_THETA ** (-jnp.arange(half, dtype=F32) / half)
    ang = jnp.arange(seq, dtype=F32)[:, None] * inv_freq[None, :]
    cos, sin = jnp.cos(ang), jnp.sin(ang)
    reps = LANES // HEAD_DIM
    cos_t = jnp.tile(jnp.concatenate([cos, cos], axis=1), (1, reps))
    sin_t = jnp.tile(jnp.concatenate([-sin, sin], axis=1), (1, reps))
    return cos_t, sin_t


def _resident(shape):
    return pl.BlockSpec(shape, lambda t: (0,) * len(shape), pipeline_mode=pl.Buffered(1))


@jax.jit
def kernel(x, attn_norm_g, w_in, attn_sinks, w_pool, pool_scale, w_out, mlp_norm_g, w_up,
           w_down, final_norm_g):
    B, S, D = x.shape
    assert D == D_MODEL and S % SEQ_TILE == 0 and attn_norm_g.shape[0] == 1
    T = SEQ_TILE
    tiles_per_seq = S // T
    n_tiles = B * tiles_per_seq
    cos_t, sin_t = _rope_tables(S)
    row = lambda v: v.reshape(1, -1).astype(F32)

    def mix_tile(t):
        n = jnp.minimum(t, n_tiles - 1)
        return n // tiles_per_seq, n % tiles_per_seq

    def mlp_tile(t):
        n = jnp.maximum(t - 1, 0)
        return n // tiles_per_seq, n % tiles_per_seq

    grid_spec = pl.GridSpec(
        grid=(n_tiles + 1,),
        in_specs=[
            pl.BlockSpec(memory_space=pltpu.SMEM),
            pl.BlockSpec((None, T, D), lambda t: (*mix_tile(t), 0)),
            pl.BlockSpec((T, LANES), lambda t: (mix_tile(t)[1], 0)),
            pl.BlockSpec((T, LANES), lambda t: (mix_tile(t)[1], 0)),
            _resident((1, D)),
            _resident((D, IN_WIDTH)),
            _resident((N_POOL_GROUPS, POOL_GROUP_DIM, POOL_GROUP_DIM)),
            _resident((1, POOL_WIDTH)),
            _resident((D, D)),
            _resident((1, D)),
            _resident((D, D_FF)),
            _resident((D_FF, D)),
            _resident((1, D)),
        ],
        out_specs=pl.BlockSpec((None, T, D), lambda t: (*mlp_tile(t), 0)),
        scratch_shapes=[
            pltpu.VMEM((N_KV_HEADS, BLOCK, LANES), BF16),
            pltpu.VMEM((KV_WIDTH, BLOCK), BF16),
            pltpu.VMEM((POOL_HALO, POOL_WIDTH), F32),
            pltpu.VMEM((T, D), BF16),
            pltpu.VMEM((T, D), F32),
            pltpu.VMEM((D, D), BF16),
        ],
    )
    return pl.pallas_call(
        functools.partial(_block_kernel, tiles_per_seq),
        grid_spec=grid_spec,
        out_shape=jax.ShapeDtypeStruct((B, S, D), x.dtype),
        compiler_params=pltpu.CompilerParams(
            dimension_semantics=("arbitrary",),
            vmem_limit_bytes=VMEM_LIMIT_BYTES,
        ),
        name="hymba_block",
    )(attn_sinks[0].astype(F32), x, cos_t, sin_t, row(attn_norm_g[0]), w_in[0].astype(BF16),
      w_pool[0], row(pool_scale[0]), w_out[0], row(mlp_norm_g[0]),
      w_up[0].astype(BF16), w_down[0].astype(BF16), row(final_norm_g))
```

```python
import functools

import jax
import jax.numpy as jnp
from jax import lax
from jax.experimental import pallas as pl
from jax.experimental.pallas import tpu as pltpu

D_MODEL = 1024
HEAD_DIM = 64
N_Q_HEADS = 8
N_KV_HEADS = 2
Q_PER_KV = N_Q_HEADS // N_KV_HEADS
ATTN_WIDTH = N_Q_HEADS * HEAD_DIM
KV_WIDTH = N_KV_HEADS * HEAD_DIM
WINDOW = 128
BLOCK = WINDOW
ROPE_THETA = 10000.0
POOL_WINDOWS = (2, 4, 8, 16)
N_POOL_GROUPS = len(POOL_WINDOWS)
POOL_WIDTH = D_MODEL - ATTN_WIDTH
POOL_GROUP_DIM = POOL_WIDTH // N_POOL_GROUPS
POOL_HALO = 16
IN_WIDTH = ATTN_WIDTH + 2 * KV_WIDTH + POOL_WIDTH
D_FF = 4 * D_MODEL
EPS = 1e-6

LANES = 128
SEQ_TILE = 512
FF_CHUNK = 1024
VMEM_LIMIT_BYTES = 56 * 1024 * 1024

F32 = jnp.float32
BF16 = jnp.bfloat16


def _rmsnorm(x, g):
    ms = jnp.mean(x * x, axis=-1, keepdims=True)
    return x * lax.rsqrt(ms + EPS) * g


def _dot(a, b):
    return jnp.dot(a, b, preferred_element_type=F32)


def _rope(xc, cos, sin_signed, first_half):
    swapped = jnp.where(first_half, pltpu.roll(xc, LANES - HEAD_DIM // 2, 1),
                        pltpu.roll(xc, HEAD_DIM // 2, 1))
    return xc * cos + swapped * sin_signed


def _dup_heads(t, low_half):
    r = pltpu.roll(t, HEAD_DIM, 1)
    return jnp.where(low_half, t, r), jnp.where(low_half, r, t)


def _token_mixing(s, sink_ref, x_ref, cos_ref, sin_ref, g1_ref, win_ref, woutc_ref,
                  kprev_ref, vprev_ref, uprev_ref, mix_ref, x1_ref):
    T = SEQ_TILE
    x = x_ref[...]
    h = _rmsnorm(x, g1_ref[...]).astype(BF16)
    proj = _dot(h, win_ref[...])
    yield

    lane = lax.broadcasted_iota(jnp.int32, (1, LANES), 1)
    low_half = lane < HEAD_DIM
    first_half = (lane % HEAD_DIM) < (HEAD_DIM // 2)
    cos = cos_ref[...]
    sin_signed = sin_ref[...]

    scale = HEAD_DIM ** -0.5
    q_heads = []
    for c in range(ATTN_WIDTH // LANES):
        qc = _rope(proj[:, c * LANES:(c + 1) * LANES], cos, sin_signed, first_half) * scale
        q_heads.append(jnp.where(low_half, qc, 0.0).astype(BF16))
        q_heads.append(jnp.where(low_half, 0.0, qc).astype(BF16))
    k_r = _rope(proj[:, ATTN_WIDTH:ATTN_WIDTH + KV_WIDTH], cos, sin_signed, first_half)
    k_dup = [t.astype(BF16) for t in _dup_heads(k_r, low_half)]
    k_ext = [jnp.concatenate([kprev_ref[g], k_dup[g]], axis=0) for g in range(N_KV_HEADS)]
    v_t = proj[:, ATTN_WIDTH + KV_WIDTH:ATTN_WIDTH + 2 * KV_WIDTH].T.astype(BF16)
    vt_ext = jnp.concatenate([vprev_ref[...], v_t], axis=1)
    for g in range(N_KV_HEADS):
        kprev_ref[g] = k_dup[g][T - BLOCK:, :]
    vprev_ref[...] = v_t[:, T - BLOCK:]

    kj = lax.broadcasted_iota(jnp.int32, (2 * BLOCK, BLOCK), 0)
    qi = lax.broadcasted_iota(jnp.int32, (2 * BLOCK, BLOCK), 1)
    rel = BLOCK + qi - kj
    band = (rel >= 0) & (rel < WINDOW)
    band_first = band & ((kj >= BLOCK) | (s > 0))
    yield

    chains = [(j, g) for j in range(T // BLOCK) for g in range(N_KV_HEADS)]
    logits, probs = {}, {}
    for j, g in chains:
        rows = slice(j * BLOCK, (j + 1) * BLOCK)
        q_stack = jnp.concatenate([q_heads[g * Q_PER_KV + i][rows, :] for i in range(Q_PER_KV)], axis=0)
        k_win = k_ext[g][j * BLOCK:(j + 2) * BLOCK, :]
        logits[j, g] = lax.dot_general(k_win, q_stack, (((1,), (1,)), ((), ())),
                                       preferred_element_type=F32)
    yield
    for n, (j, g) in enumerate(chains):
        mask1 = band_first if j == 0 else band
        mask = jnp.concatenate([mask1] * Q_PER_KV, axis=1)
        sink = jnp.concatenate([jnp.full((1, BLOCK), sink_ref[g * Q_PER_KV + i], F32)
                                for i in range(Q_PER_KV)], axis=1)
        lg = jnp.where(mask, logits[j, g], -jnp.inf)
        m = jnp.maximum(jnp.max(lg, axis=0, keepdims=True), sink)
        p = jnp.exp(lg - m)
        denom = jnp.sum(p, axis=0, keepdims=True) + jnp.exp(sink - m)
        probs[j, g] = (p * (1.0 / denom)).astype(BF16)
        if n % 2 == 1:
            yield
    for j, g in chains:
        rows = slice(j * BLOCK, (j + 1) * BLOCK)
        vt_win = vt_ext[g * HEAD_DIM:(g + 1) * HEAD_DIM, j * BLOCK:(j + 2) * BLOCK]
        o_t = _dot(vt_win, probs[j, g])
        for cc in range(Q_PER_KV // 2):
            pair = jnp.concatenate([o_t[:, (2 * cc) * BLOCK:(2 * cc + 1) * BLOCK],
                                    o_t[:, (2 * cc + 1) * BLOCK:(2 * cc + 2) * BLOCK]], axis=0)
            c = g * (Q_PER_KV // 2) + cc
            mix_ref[rows, c * LANES:(c + 1) * LANES] = pair.T.astype(BF16)
    yield

    u = proj[:, ATTN_WIDTH + 2 * KV_WIDTH:]
    u_ext = jnp.concatenate([uprev_ref[...], u], axis=0)
    uprev_ref[...] = u[T - POOL_HALO:, :]
    tpos = s * T + lax.broadcasted_iota(jnp.int32, (T, 1), 0)
    for g, w in enumerate(POOL_WINDOWS):
        cols = slice(g * POOL_GROUP_DIM, (g + 1) * POOL_GROUP_DIM)
        acc = u_ext[:, cols]
        step = 1
        while step < w:
            acc = acc + pltpu.roll(acc, step, 0)
            step *= 2
        wsum = acc[POOL_HALO:, :]
        cnt = jnp.minimum(tpos + 1, w).astype(F32)
        d = (wsum / cnt - u[:, cols]).astype(BF16)
        mix_ref[:, ATTN_WIDTH + g * POOL_GROUP_DIM:ATTN_WIDTH + (g + 1) * POOL_GROUP_DIM] = d
    yield

    x1_ref[...] = x + _dot(mix_ref[...], woutc_ref[...])


def _fold_kernel(wpool_ref, pscale_ref, wout_ref, woutc_ref):
    woutc_ref[0:ATTN_WIDTH, :] = wout_ref[0:ATTN_WIDTH, :].astype(BF16)
    for g in range(N_POOL_GROUPS):
        cols = slice(g * POOL_GROUP_DIM, (g + 1) * POOL_GROUP_DIM)
        rows = slice(ATTN_WIDTH + g * POOL_GROUP_DIM, ATTN_WIDTH + (g + 1) * POOL_GROUP_DIM)
        scaled = wpool_ref[g] * pscale_ref[:, cols]
        folded = jnp.dot(scaled, wout_ref[rows, :], precision=lax.Precision.HIGHEST,
                         preferred_element_type=F32)
        woutc_ref[rows, :] = folded.astype(BF16)


def _fold_out_projection(w_pool, pool_scale, w_out):
    return pl.pallas_call(
        _fold_kernel,
        out_shape=jax.ShapeDtypeStruct(w_out.shape, BF16),
        name="fold_out_projection",
    )(w_pool, pool_scale, w_out)


def _channel_mixing(x1_ref, g2_ref, wup_ref, wdown_ref, gf_ref, o_ref):
    x1 = x1_ref[...]
    h2 = _rmsnorm(x1, g2_ref[...]).astype(BF16)
    yield
    acc = jnp.zeros((SEQ_TILE, D_MODEL), F32)
    for c in range(D_FF // FF_CHUNK):
        up = _dot(h2, wup_ref[:, c * FF_CHUNK:(c + 1) * FF_CHUNK])
        a = jnp.square(jnp.maximum(up, 0.0)).astype(BF16)
        yield
        acc = acc + _dot(a, wdown_ref[c * FF_CHUNK:(c + 1) * FF_CHUNK, :])
        yield
    x2 = x1 + acc
    o_ref[...] = _rmsnorm(x2, gf_ref[...])


def _block_kernel(tiles_per_seq, sink_ref, x_ref, cos_ref, sin_ref, g1_ref, win_ref, woutc_ref,
                  g2_ref, wup_ref, wdown_ref, gf_ref, o_ref,
                  kprev_ref, vprev_ref, uprev_ref, mix_ref, x1_ref):
    t = pl.program_id(0)
    n_tiles = pl.num_programs(0) - 1
    s = jnp.minimum(t, n_tiles - 1) % tiles_per_seq

    @pl.when(t == 0)
    def _():
        x1_ref[...] = jnp.zeros_like(x1_ref)

    @pl.when(s == 0)
    def _():
        kprev_ref[...] = jnp.zeros_like(kprev_ref)
        vprev_ref[...] = jnp.zeros_like(vprev_ref)
        uprev_ref[...] = jnp.zeros_like(uprev_ref)

    halves = [
        _channel_mixing(x1_ref, g2_ref, wup_ref, wdown_ref, gf_ref, o_ref),
        _token_mixing(s, sink_ref, x_ref, cos_ref, sin_ref, g1_ref, win_ref, woutc_ref,
                      kprev_ref, vprev_ref, uprev_ref, mix_ref, x1_ref),
    ]
    while halves:
        for gen in list(halves):
            if next(gen, StopIteration) is StopIteration:
                halves.remove(gen)


def _rope_tables(seq):
    half = HEAD_DIM // 2
    inv_freq = ROPE_THETA ** (-jnp.arange(half, dtype=F32) / half)
    ang = jnp.arange(seq, dtype=F32)[:, None] * inv_freq[None, :]
    cos, sin = jnp.cos(ang), jnp.sin(ang)
    reps = LANES // HEAD_DIM
    cos_t = jnp.tile(jnp.concatenate([cos, cos], axis=1), (1, reps))
    sin_t = jnp.tile(jnp.concatenate([-sin, sin], axis=1), (1, reps))
    return cos_t, sin_t


def _resident(shape):
    return pl.BlockSpec(shape, lambda t: (0,) * len(shape), pipeline_mode=pl.Buffered(1))


@jax.jit
def kernel(x, attn_norm_g, w_in, attn_sinks, w_pool, pool_scale, w_out, mlp_norm_g, w_up,
           w_down, final_norm_g):
    B, S, D = x.shape
    assert D == D_MODEL and S % SEQ_TILE == 0 and attn_norm_g.shape[0] == 1
    T = SEQ_TILE
    tiles_per_seq = S // T
    n_tiles = B * tiles_per_seq
    cos_t, sin_t = _rope_tables(S)
    row = lambda v: v.reshape(1, -1).astype(F32)

    def mix_tile(t):
        n = jnp.minimum(t, n_tiles - 1)
        return n // tiles_per_seq, n % tiles_per_seq

    def mlp_tile(t):
        n = jnp.maximum(t - 1, 0)
        return n // tiles_per_seq, n % tiles_per_seq

    grid_spec = pl.GridSpec(
        grid=(n_tiles + 1,),
        in_specs=[
            pl.BlockSpec(memory_space=pltpu.SMEM),
            pl.BlockSpec((None, T, D), lambda t: (*mix_tile(t), 0)),
            pl.BlockSpec((T, LANES), lambda t: (mix_tile(t)[1], 0)),
            pl.BlockSpec((T, LANES), lambda t: (mix_tile(t)[1], 0)),
            _resident((1, D)),
            _resident((D, IN_WIDTH)),
            _resident((D, D)),
            _resident((1, D)),
            _resident((D, D_FF)),
            _resident((D_FF, D)),
            _resident((1, D)),
        ],
        out_specs=pl.BlockSpec((None, T, D), lambda t: (*mlp_tile(t), 0)),
        scratch_shapes=[
            pltpu.VMEM((N_KV_HEADS, BLOCK, LANES), BF16),
            pltpu.VMEM((KV_WIDTH, BLOCK), BF16),
            pltpu.VMEM((POOL_HALO, POOL_WIDTH), F32),
            pltpu.VMEM((T, D), BF16),
            pltpu.VMEM((T, D), F32),
        ],
    )
    return pl.pallas_call(
        functools.partial(_block_kernel, tiles_per_seq),
        grid_spec=grid_spec,
        out_shape=jax.ShapeDtypeStruct((B, S, D), x.dtype),
        compiler_params=pltpu.CompilerParams(
            dimension_semantics=("arbitrary",),
            vmem_limit_bytes=VMEM_LIMIT_BYTES,
        ),
        name="hymba_block",
    )(attn_sinks[0].astype(F32), x, cos_t, sin_t, row(attn_norm_g[0]), w_in[0].astype(BF16),
      _fold_out_projection(w_pool[0], row(pool_scale[0]), w_out[0]), row(mlp_norm_g[0]),
      w_up[0].astype(BF16), w_down[0].astype(BF16), row(final_norm_g))
```

```python
import functools

import jax
import jax.numpy as jnp
from jax import lax
from jax.experimental import pallas as pl
from jax.experimental.pallas import tpu as pltpu

D_MODEL = 1024
HEAD_DIM = 64
N_Q_HEADS = 8
N_KV_HEADS = 2
Q_PER_KV = N_Q_HEADS // N_KV_HEADS
ATTN_WIDTH = N_Q_HEADS * HEAD_DIM
KV_WIDTH = N_KV_HEADS * HEAD_DIM
WINDOW = 128
BLOCK = WINDOW
ROPE_THETA = 10000.0
POOL_WINDOWS = (2, 4, 8, 16)
N_POOL_GROUPS = len(POOL_WINDOWS)
POOL_WIDTH = D_MODEL - ATTN_WIDTH
POOL_GROUP_DIM = POOL_WIDTH // N_POOL_GROUPS
POOL_HALO = 16
IN_WIDTH = ATTN_WIDTH + 2 * KV_WIDTH + POOL_WIDTH
D_FF = 4 * D_MODEL
EPS = 1e-6

LANES = 128
SEQ_TILE = 512
TILES_PER_STEP = 2
STEP_ROWS = SEQ_TILE * TILES_PER_STEP
FF_CHUNK = 1024
VMEM_LIMIT_BYTES = 60 * 1024 * 1024

F32 = jnp.float32
BF16 = jnp.bfloat16


def _rmsnorm(x, g):
    ms = jnp.mean(x * x, axis=-1, keepdims=True)
    return x * lax.rsqrt(ms + EPS) * g


def _dot(a, b):
    return jnp.dot(a, b, preferred_element_type=F32)


def _rope(xc, cos, sin_signed, first_half):
    swapped = jnp.where(first_half, pltpu.roll(xc, LANES - HEAD_DIM // 2, 1),
                        pltpu.roll(xc, HEAD_DIM // 2, 1))
    return xc * cos + swapped * sin_signed


def _dup_heads(t, low_half):
    r = pltpu.roll(t, HEAD_DIM, 1)
    return jnp.where(low_half, t, r), jnp.where(low_half, r, t)


def _token_mixing(s, rows, sink_ref, x_ref, cos_ref, sin_ref, g1_ref, win_ref, wpool_ref, pscale_ref,
                  wout_ref, kprev_ref, vprev_ref, uprev_ref, mix_ref, x1_ref):
    T = SEQ_TILE
    x = x_ref[rows, :]
    h = _rmsnorm(x, g1_ref[...]).astype(BF16)
    proj = _dot(h, win_ref[...])
    yield

    lane = lax.broadcasted_iota(jnp.int32, (1, LANES), 1)
    low_half = lane < HEAD_DIM
    first_half = (lane % HEAD_DIM) < (HEAD_DIM // 2)
    cos = cos_ref[rows, :]
    sin_signed = sin_ref[rows, :]

    scale = HEAD_DIM ** -0.5
    q_heads = []
    for c in range(ATTN_WIDTH // LANES):
        qc = _rope(proj[:, c * LANES:(c + 1) * LANES], cos, sin_signed, first_half) * scale
        q_heads.append(jnp.where(low_half, qc, 0.0).astype(BF16))
        q_heads.append(jnp.where(low_half, 0.0, qc).astype(BF16))
    k_r = _rope(proj[:, ATTN_WIDTH:ATTN_WIDTH + KV_WIDTH], cos, sin_signed, first_half)
    k_dup = [t.astype(BF16) for t in _dup_heads(k_r, low_half)]
    k_ext = [jnp.concatenate([kprev_ref[g], k_dup[g]], axis=0) for g in range(N_KV_HEADS)]
    v_t = proj[:, ATTN_WIDTH + KV_WIDTH:ATTN_WIDTH + 2 * KV_WIDTH].T.astype(BF16)
    vt_ext = jnp.concatenate([vprev_ref[...], v_t], axis=1)
    for g in range(N_KV_HEADS):
        kprev_ref[g] = k_dup[g][T - BLOCK:, :]
    vprev_ref[...] = v_t[:, T - BLOCK:]

    kj = lax.broadcasted_iota(jnp.int32, (2 * BLOCK, BLOCK), 0)
    qi = lax.broadcasted_iota(jnp.int32, (2 * BLOCK, BLOCK), 1)
    rel = BLOCK + qi - kj
    band = (rel >= 0) & (rel < WINDOW)
    band_first = band & ((kj >= BLOCK) | (s > 0))
    yield

    chains = [(j, g) for j in range(T // BLOCK) for g in range(N_KV_HEADS)]
    logits, probs = {}, {}
    for j, g in chains:
        blk = slice(j * BLOCK, (j + 1) * BLOCK)
        q_stack = jnp.concatenate([q_heads[g * Q_PER_KV + i][blk, :] for i in range(Q_PER_KV)], axis=0)
        k_win = k_ext[g][j * BLOCK:(j + 2) * BLOCK, :]
        logits[j, g] = lax.dot_general(k_win, q_stack, (((1,), (1,)), ((), ())),
                                       preferred_element_type=F32)
    yield
    for n, (j, g) in enumerate(chains):
        mask1 = band_first if j == 0 else band
        mask = jnp.concatenate([mask1] * Q_PER_KV, axis=1)
        sink = jnp.concatenate([jnp.full((1, BLOCK), sink_ref[g * Q_PER_KV + i], F32)
                                for i in range(Q_PER_KV)], axis=1)
        lg = jnp.where(mask, logits[j, g], -jnp.inf)
        m = jnp.maximum(jnp.max(lg, axis=0, keepdims=True), sink)
        p = jnp.exp(lg - m)
        denom = jnp.sum(p, axis=0, keepdims=True) + jnp.exp(sink - m)
        probs[j, g] = (p * (1.0 / denom)).astype(BF16)
        if n % 2 == 1:
            yield
    for j, g in chains:
        blk = slice(j * BLOCK, (j + 1) * BLOCK)
        vt_win = vt_ext[g * HEAD_DIM:(g + 1) * HEAD_DIM, j * BLOCK:(j + 2) * BLOCK]
        o_t = _dot(vt_win, probs[j, g])
        for cc in range(Q_PER_KV // 2):
            pair = jnp.concatenate([o_t[:, (2 * cc) * BLOCK:(2 * cc + 1) * BLOCK],
                                    o_t[:, (2 * cc + 1) * BLOCK:(2 * cc + 2) * BLOCK]], axis=0)
            c = g * (Q_PER_KV // 2) + cc
            mix_ref[blk, c * LANES:(c + 1) * LANES] = pair.T.astype(BF16)
    yield

    u = proj[:, ATTN_WIDTH + 2 * KV_WIDTH:]
    u_ext = jnp.concatenate([uprev_ref[...], u], axis=0)
    uprev_ref[...] = u[T - POOL_HALO:, :]
    tpos = s * T + lax.broadcasted_iota(jnp.int32, (T, 1), 0)
    pscale = pscale_ref[...]
    for g, w in enumerate(POOL_WINDOWS):
        cols = slice(g * POOL_GROUP_DIM, (g + 1) * POOL_GROUP_DIM)
        acc = u_ext[:, cols]
        step = 1
        while step < w:
            acc = acc + pltpu.roll(acc, step, 0)
            step *= 2
        wsum = acc[POOL_HALO:, :]
        cnt = jnp.minimum(tpos + 1, w).astype(F32)
        d = (wsum / cnt - u[:, cols]).astype(BF16)
        y = _dot(d, wpool_ref[g]) * pscale[:, cols]
        mix_ref[:, ATTN_WIDTH + g * POOL_GROUP_DIM:ATTN_WIDTH + (g + 1) * POOL_GROUP_DIM] = y.astype(BF16)
    yield

    x1_ref[...] = x + _dot(mix_ref[...], wout_ref[...])


def _channel_mixing(rows, x1_ref, g2_ref, wup_ref, wdown_ref, gf_ref, o_ref):
    x1 = x1_ref[...]
    h2 = _rmsnorm(x1, g2_ref[...]).astype(BF16)
    yield
    acc = jnp.zeros((SEQ_TILE, D_MODEL), F32)
    for c in range(D_FF // FF_CHUNK):
        up = _dot(h2, wup_ref[:, c * FF_CHUNK:(c + 1) * FF_CHUNK])
        a = jnp.square(jnp.maximum(up, 0.0)).astype(BF16)
        yield
        acc = acc + _dot(a, wdown_ref[c * FF_CHUNK:(c + 1) * FF_CHUNK, :])
        yield
    x2 = x1 + acc
    o_ref[rows, :] = _rmsnorm(x2, gf_ref[...])


def _block_kernel(steps_per_seq, sink_ref, x_ref, cos_ref, sin_ref, g1_ref, win_ref, wpool_ref,
                  pscale_ref, wout_ref, g2_ref, wup_ref, wdown_ref, gf_ref, o_ref,
                  kprev_ref, vprev_ref, uprev_ref, mix_ref, x1_ref):
    t = pl.program_id(0)
    n_steps = pl.num_programs(0) - 1
    seq_step = jnp.minimum(t, n_steps - 1) % steps_per_seq

    @pl.when(t == 0)
    def _():
        x1_ref[...] = jnp.zeros_like(x1_ref)

    @pl.when(seq_step == 0)
    def _():
        kprev_ref[...] = jnp.zeros_like(kprev_ref)
        vprev_ref[...] = jnp.zeros_like(vprev_ref)
        uprev_ref[...] = jnp.zeros_like(uprev_ref)

    for slot in range(TILES_PER_STEP):
        rows = slice(slot * SEQ_TILE, (slot + 1) * SEQ_TILE)
        s = seq_step * TILES_PER_STEP + slot
        halves = [
            _channel_mixing(rows, x1_ref.at[slot], g2_ref, wup_ref, wdown_ref, gf_ref, o_ref),
            _token_mixing(s, rows, sink_ref, x_ref, cos_ref, sin_ref, g1_ref, win_ref, wpool_ref,
                          pscale_ref, wout_ref, kprev_ref, vprev_ref, uprev_ref, mix_ref,
                          x1_ref.at[slot]),
        ]
        while halves:
            for gen in list(halves):
                if next(gen, StopIteration) is StopIteration:
                    halves.remove(gen)


def _rope_tables(seq):
    half = HEAD_DIM // 2
    inv_freq = ROPE_THETA ** (-jnp.arange(half, dtype=F32) / half)
    ang = jnp.arange(seq, dtype=F32)[:, None] * inv_freq[None, :]
    cos, sin = jnp.cos(ang), jnp.sin(ang)
    reps = LANES // HEAD_DIM
    cos_t = jnp.tile(jnp.concatenate([cos, cos], axis=1), (1, reps))
    sin_t = jnp.tile(jnp.concatenate([-sin, sin], axis=1), (1, reps))
    return cos_t, sin_t


def _resident(shape):
    return pl.BlockSpec(shape, lambda t: (0,) * len(shape), pipeline_mode=pl.Buffered(1))


@jax.jit
def kernel(x, attn_norm_g, w_in, attn_sinks, w_pool, pool_scale, w_out, mlp_norm_g, w_up,
           w_down, final_norm_g):
    B, S, D = x.shape
    assert D == D_MODEL and S % STEP_ROWS == 0 and attn_norm_g.shape[0] == 1
    steps_per_seq = S // STEP_ROWS
    n_steps = B * steps_per_seq
    cos_t, sin_t = _rope_tables(S)
    row = lambda v: v.reshape(1, -1).astype(F32)

    def mix_block(t):
        n = jnp.minimum(t, n_steps - 1)
        return n // steps_per_seq, n % steps_per_seq

    def mlp_block(t):
        n = jnp.maximum(t - 1, 0)
        return n // steps_per_seq, n % steps_per_seq

    grid_spec = pl.GridSpec(
        grid=(n_steps + 1,),
        in_specs=[
            pl.BlockSpec(memory_space=pltpu.SMEM),
            pl.BlockSpec((None, STEP_ROWS, D), lambda t: (*mix_block(t), 0)),
            pl.BlockSpec((STEP_ROWS, LANES), lambda t: (mix_block(t)[1], 0)),
            pl.BlockSpec((STEP_ROWS, LANES), lambda t: (mix_block(t)[1], 0)),
            _resident((1, D)),
            _resident((D, IN_WIDTH)),
            _resident((N_POOL_GROUPS, POOL_GROUP_DIM, POOL_GROUP_DIM)),
            _resident((1, POOL_WIDTH)),
            _resident((D, D)),
            _resident((1, D)),
            _resident((D, D_FF)),
            _resident((D_FF, D)),
            _resident((1, D)),
        ],
        out_specs=pl.BlockSpec((None, STEP_ROWS, D), lambda t: (*mlp_block(t), 0)),
        scratch_shapes=[
            pltpu.VMEM((N_KV_HEADS, BLOCK, LANES), BF16),
            pltpu.VMEM((KV_WIDTH, BLOCK), BF16),
            pltpu.VMEM((POOL_HALO, POOL_WIDTH), F32),
            pltpu.VMEM((SEQ_TILE, D), BF16),
            pltpu.VMEM((TILES_PER_STEP, SEQ_TILE, D), F32),
        ],
    )
    return pl.pallas_call(
        functools.partial(_block_kernel, steps_per_seq),
        grid_spec=grid_spec,
        out_shape=jax.ShapeDtypeStruct((B, S, D), x.dtype),
        compiler_params=pltpu.CompilerParams(
            dimension_semantics=("arbitrary",),
            vmem_limit_bytes=VMEM_LIMIT_BYTES,
        ),
        name="hymba_block",
    )(attn_sinks[0].astype(F32), x, cos_t, sin_t, row(attn_norm_g[0]), w_in[0].astype(BF16),
      w_pool[0].astype(BF16), row(pool_scale[0]), w_out[0].astype(BF16), row(mlp_norm_g[0]),
      w_up[0].astype(BF16), w_down[0].astype(BF16), row(final_norm_g))
```

```python
import functools

import jax
import jax.numpy as jnp
import numpy as np
from jax import lax
from jax.experimental import pallas as pl
from jax.experimental.pallas import tpu as pltpu

D_MODEL = 1024
HEAD_DIM = 64
N_Q_HEADS = 8
N_KV_HEADS = 2
Q_PER_KV = N_Q_HEADS // N_KV_HEADS
ATTN_WIDTH = N_Q_HEADS * HEAD_DIM
KV_WIDTH = N_KV_HEADS * HEAD_DIM
WINDOW = 128
BLOCK = WINDOW
ROPE_THETA = 10000.0
POOL_WINDOWS = (2, 4, 8, 16)
N_POOL_GROUPS = len(POOL_WINDOWS)
POOL_WIDTH = D_MODEL - ATTN_WIDTH
POOL_GROUP_DIM = POOL_WIDTH // N_POOL_GROUPS
POOL_HALO = 16
IN_WIDTH = ATTN_WIDTH + 2 * KV_WIDTH + POOL_WIDTH
D_FF = 4 * D_MODEL
EPS = 1e-6

LANES = 128
SEQ_TILE = 512
TILES_PER_STEP = 2
STEP_ROWS = SEQ_TILE * TILES_PER_STEP
FF_CHUNK = 1024
VMEM_LIMIT_BYTES = 60 * 1024 * 1024

F32 = jnp.float32
BF16 = jnp.bfloat16


def _rmsnorm(x, g):
    ms = jnp.mean(x * x, axis=-1, keepdims=True)
    return x * lax.rsqrt(ms + EPS) * g


def _dot(a, b):
    return jnp.dot(a, b, preferred_element_type=F32)


def _rope(xc, cos, sin_signed, first_half):
    swapped = jnp.where(first_half, pltpu.roll(xc, LANES - HEAD_DIM // 2, 1),
                        pltpu.roll(xc, HEAD_DIM // 2, 1))
    return xc * cos + swapped * sin_signed


def _dup_heads(t, low_half):
    r = pltpu.roll(t, HEAD_DIM, 1)
    return jnp.where(low_half, t, r), jnp.where(low_half, r, t)


def _token_mixing(s, rows, sink_ref, x_ref, cos_ref, sin_ref, g1_ref, win_ref, wpool_ref, pscale_ref,
                  wout_ref, kprev_ref, vprev_ref, uprev_ref, mix_ref, x1_ref):
    T = SEQ_TILE
    x = x_ref[rows, :]
    h = _rmsnorm(x, g1_ref[...]).astype(BF16)
    proj = _dot(h, win_ref[...])
    yield

    lane = lax.broadcasted_iota(jnp.int32, (1, LANES), 1)
    low_half = lane < HEAD_DIM
    first_half = (lane % HEAD_DIM) < (HEAD_DIM // 2)
    cos = cos_ref[rows, :]
    sin_signed = sin_ref[rows, :]

    scale = HEAD_DIM ** -0.5
    q_heads = []
    for c in range(ATTN_WIDTH // LANES):
        qc = _rope(proj[:, c * LANES:(c + 1) * LANES], cos, sin_signed, first_half) * scale
        q_heads.append(jnp.where(low_half, qc, 0.0).astype(BF16))
        q_heads.append(jnp.where(low_half, 0.0, qc).astype(BF16))
    k_r = _rope(proj[:, ATTN_WIDTH:ATTN_WIDTH + KV_WIDTH], cos, sin_signed, first_half)
    k_dup = [t.astype(BF16) for t in _dup_heads(k_r, low_half)]
    k_ext = [jnp.concatenate([kprev_ref[g], k_dup[g]], axis=0) for g in range(N_KV_HEADS)]
    v_t = proj[:, ATTN_WIDTH + KV_WIDTH:ATTN_WIDTH + 2 * KV_WIDTH].T.astype(BF16)
    vt_ext = jnp.concatenate([vprev_ref[...], v_t], axis=1)
    for g in range(N_KV_HEADS):
        kprev_ref[g] = k_dup[g][T - BLOCK:, :]
    vprev_ref[...] = v_t[:, T - BLOCK:]

    kj = lax.broadcasted_iota(jnp.int32, (2 * BLOCK, BLOCK), 0)
    qi = lax.broadcasted_iota(jnp.int32, (2 * BLOCK, BLOCK), 1)
    rel = BLOCK + qi - kj
    band = (rel >= 0) & (rel < WINDOW)
    band_first = band & ((kj >= BLOCK) | (s > 0))
    yield

    logits = {}
    for j in range(T // BLOCK):
        for g in range(N_KV_HEADS):
            blk = slice(j * BLOCK, (j + 1) * BLOCK)
            q_stack = jnp.concatenate([q_heads[g * Q_PER_KV + i][blk, :] for i in range(Q_PER_KV)], axis=0)
            k_win = k_ext[g][j * BLOCK:(j + 2) * BLOCK, :]
            logits[j, g] = lax.dot_general(k_win, q_stack, (((1,), (1,)), ((), ())),
                                           preferred_element_type=F32)
    yield
    for j in range(T // BLOCK):
        probs = {}
        for g in range(N_KV_HEADS):
            mask1 = band_first if j == 0 else band
            mask = jnp.concatenate([mask1] * Q_PER_KV, axis=1)
            sink = jnp.concatenate([jnp.full((1, BLOCK), sink_ref[g * Q_PER_KV + i], F32)
                                    for i in range(Q_PER_KV)], axis=1)
            lg = jnp.where(mask, logits[j, g], -jnp.inf)
            m = jnp.maximum(jnp.max(lg, axis=0, keepdims=True), sink)
            p = jnp.exp(lg - m)
            denom = jnp.sum(p, axis=0, keepdims=True) + jnp.exp(sink - m)
            probs[g] = (p * (1.0 / denom)).astype(BF16)
        yield
        blk = slice(j * BLOCK, (j + 1) * BLOCK)
        for g in range(N_KV_HEADS):
            vt_win = vt_ext[g * HEAD_DIM:(g + 1) * HEAD_DIM, j * BLOCK:(j + 2) * BLOCK]
            o_t = _dot(vt_win, probs[g])
            for cc in range(Q_PER_KV // 2):
                pair = jnp.concatenate([o_t[:, (2 * cc) * BLOCK:(2 * cc + 1) * BLOCK],
                                        o_t[:, (2 * cc + 1) * BLOCK:(2 * cc + 2) * BLOCK]], axis=0)
                c = g * (Q_PER_KV // 2) + cc
                mix_ref[blk, c * LANES:(c + 1) * LANES] = pair.T.astype(BF16)
        yield

    u = proj[:, ATTN_WIDTH + 2 * KV_WIDTH:]
    u_ext = jnp.concatenate([uprev_ref[...], u], axis=0)
    uprev_ref[...] = u[T - POOL_HALO:, :]
    tpos = s * T + lax.broadcasted_iota(jnp.int32, (T, 1), 0)
    pscale = pscale_ref[...]
    for g, w in enumerate(POOL_WINDOWS):
        cols = slice(g * POOL_GROUP_DIM, (g + 1) * POOL_GROUP_DIM)
        acc = u_ext[:, cols]
        step = 1
        while step < w:
            acc = acc + pltpu.roll(acc, step, 0)
            step *= 2
        wsum = acc[POOL_HALO:, :]
        cnt = jnp.minimum(tpos + 1, w).astype(F32)
        d = (wsum / cnt - u[:, cols]).astype(BF16)
        y = _dot(d, wpool_ref[g]) * pscale[:, cols]
        mix_ref[:, ATTN_WIDTH + g * POOL_GROUP_DIM:ATTN_WIDTH + (g + 1) * POOL_GROUP_DIM] = y.astype(BF16)
    yield

    x1_ref[...] = x + _dot(mix_ref[...], wout_ref[...])


def _channel_mixing(rows, x1_ref, g2_ref, wup_ref, wdown_ref, gf_ref, o_ref):
    x1 = x1_ref[...]
    h2 = _rmsnorm(x1, g2_ref[...]).astype(BF16)
    yield
    acc = jnp.zeros((SEQ_TILE, D_MODEL), F32)
    n_chunks = D_FF // FF_CHUNK
    for c in range(n_chunks):
        acts = []
        for lo in (c * FF_CHUNK, c * FF_CHUNK + FF_CHUNK // 2):
            up = _dot(h2, wup_ref[:, lo:lo + FF_CHUNK // 2])
            acts.append(jnp.square(jnp.maximum(up, 0.0)).astype(BF16))
            yield
        acc = acc + _dot(jnp.concatenate(acts, axis=1), wdown_ref[c * FF_CHUNK:(c + 1) * FF_CHUNK, :])
        if c < n_chunks - 1:
            yield
    x2 = x1 + acc
    o_ref[rows, :] = _rmsnorm(x2, gf_ref[...])


def _block_kernel(steps_per_seq, sink_ref, x_ref, cos_ref, sin_ref, g1_ref, win_ref, wpool_ref,
                  pscale_ref, wout_ref, g2_ref, wup_ref, wdown_ref, gf_ref, o_ref,
                  kprev_ref, vprev_ref, uprev_ref, mix_ref, x1_ref):
    t = pl.program_id(0)
    n_steps = pl.num_programs(0) - 1
    seq_step = jnp.minimum(t, n_steps - 1) % steps_per_seq

    @pl.when(t == 0)
    def _():
        x1_ref[...] = jnp.zeros_like(x1_ref)

    @pl.when(seq_step == 0)
    def _():
        kprev_ref[...] = jnp.zeros_like(kprev_ref)
        vprev_ref[...] = jnp.zeros_like(vprev_ref)
        uprev_ref[...] = jnp.zeros_like(uprev_ref)

    for slot in range(TILES_PER_STEP):
        rows = slice(slot * SEQ_TILE, (slot + 1) * SEQ_TILE)
        s = seq_step * TILES_PER_STEP + slot
        halves = [
            _channel_mixing(rows, x1_ref.at[slot], g2_ref, wup_ref, wdown_ref, gf_ref, o_ref),
            _token_mixing(s, rows, sink_ref, x_ref, cos_ref, sin_ref, g1_ref, win_ref, wpool_ref,
                          pscale_ref, wout_ref, kprev_ref, vprev_ref, uprev_ref, mix_ref,
                          x1_ref.at[slot]),
        ]
        while halves:
            for gen in list(halves):
                if next(gen, StopIteration) is StopIteration:
                    halves.remove(gen)


def _rope_tables(seq):
    half = HEAD_DIM // 2
    inv_freq = ROPE_THETA ** (-np.arange(half, dtype=np.float64) / half)
    ang = np.arange(seq, dtype=np.float64)[:, None] * inv_freq[None, :]
    cos, sin = np.cos(ang), np.sin(ang)
    reps = LANES // HEAD_DIM
    cos_t = np.tile(np.concatenate([cos, cos], axis=1), (1, reps)).astype(np.float32)
    sin_t = np.tile(np.concatenate([-sin, sin], axis=1), (1, reps)).astype(np.float32)
    return jnp.asarray(cos_t), jnp.asarray(sin_t)


def _resident(shape):
    return pl.BlockSpec(shape, lambda t: (0,) * len(shape), pipeline_mode=pl.Buffered(1))


@jax.jit
def kernel(x, attn_norm_g, w_in, attn_sinks, w_pool, pool_scale, w_out, mlp_norm_g, w_up,
           w_down, final_norm_g):
    B, S, D = x.shape
    assert D == D_MODEL and S % STEP_ROWS == 0 and attn_norm_g.shape[0] == 1
    steps_per_seq = S // STEP_ROWS
    n_steps = B * steps_per_seq
    cos_t, sin_t = _rope_tables(S)
    row = lambda v: v.reshape(1, -1).astype(F32)

    def mix_block(t):
        n = jnp.minimum(t, n_steps - 1)
        return n // steps_per_seq, n % steps_per_seq

    def mlp_block(t):
        n = jnp.maximum(t - 1, 0)
        return n // steps_per_seq, n % steps_per_seq

    grid_spec = pl.GridSpec(
        grid=(n_steps + 1,),
        in_specs=[
            pl.BlockSpec(memory_space=pltpu.SMEM),
            pl.BlockSpec((None, STEP_ROWS, D), lambda t: (*mix_block(t), 0)),
            pl.BlockSpec((STEP_ROWS, LANES), lambda t: (mix_block(t)[1], 0)),
            pl.BlockSpec((STEP_ROWS, LANES), lambda t: (mix_block(t)[1], 0)),
            _resident((1, D)),
            _resident((D, IN_WIDTH)),
            _resident((N_POOL_GROUPS, POOL_GROUP_DIM, POOL_GROUP_DIM)),
            _resident((1, POOL_WIDTH)),
            _resident((D, D)),
            _resident((1, D)),
            _resident((D, D_FF)),
            _resident((D_FF, D)),
            _resident((1, D)),
        ],
        out_specs=pl.BlockSpec((None, STEP_ROWS, D), lambda t: (*mlp_block(t), 0)),
        scratch_shapes=[
            pltpu.VMEM((N_KV_HEADS, BLOCK, LANES), BF16),
            pltpu.VMEM((KV_WIDTH, BLOCK), BF16),
            pltpu.VMEM((POOL_HALO, POOL_WIDTH), F32),
            pltpu.VMEM((SEQ_TILE, D), BF16),
            pltpu.VMEM((TILES_PER_STEP, SEQ_TILE, D), F32),
        ],
    )
    return pl.pallas_call(
        functools.partial(_block_kernel, steps_per_seq),
        grid_spec=grid_spec,
        out_shape=jax.ShapeDtypeStruct((B, S, D), x.dtype),
        compiler_params=pltpu.CompilerParams(
            dimension_semantics=("arbitrary",),
            vmem_limit_bytes=VMEM_LIMIT_BYTES,
        ),
        name="hymba_block",
    )(attn_sinks[0].astype(F32), x, cos_t, sin_t, row(attn_norm_g[0]), w_in[0].astype(BF16),
      w_pool[0].astype(BF16), row(pool_scale[0]), w_out[0].astype(BF16), row(mlp_norm_g[0]),
      w_up[0].astype(BF16), w_down[0].astype(BF16), row(final_norm_g))
```

```python
import functools

import jax
import jax.numpy as jnp
import numpy as np
from jax import lax
from jax.experimental import pallas as pl
from jax.experimental.pallas import tpu as pltpu

D_MODEL = 1024
HEAD_DIM = 64
N_Q_HEADS = 8
N_KV_HEADS = 2
Q_PER_KV = N_Q_HEADS // N_KV_HEADS
ATTN_WIDTH = N_Q_HEADS * HEAD_DIM
KV_WIDTH = N_KV_HEADS * HEAD_DIM
WINDOW = 128
BLOCK = WINDOW
ROPE_THETA = 10000.0
POOL_WINDOWS = (2, 4, 8, 16)
N_POOL_GROUPS = len(POOL_WINDOWS)
POOL_WIDTH = D_MODEL - ATTN_WIDTH
POOL_GROUP_DIM = POOL_WIDTH // N_POOL_GROUPS
POOL_HALO = 16
IN_WIDTH = ATTN_WIDTH + 2 * KV_WIDTH + POOL_WIDTH
D_FF = 4 * D_MODEL
EPS = 1e-6

LANES = 128
SEQ_TILE = 512
TILES_PER_STEP = 2
STEP_ROWS = SEQ_TILE * TILES_PER_STEP
FF_CHUNK = 1024
VMEM_LIMIT_BYTES = 60 * 1024 * 1024

F32 = jnp.float32
BF16 = jnp.bfloat16


def _rmsnorm(x, g):
    ms = jnp.mean(x * x, axis=-1, keepdims=True)
    return x * lax.rsqrt(ms + EPS) * g


def _dot(a, b):
    return jnp.dot(a, b, preferred_element_type=F32)


def _rope(xc, cos, sin_signed, first_half):
    swapped = jnp.where(first_half, pltpu.roll(xc, LANES - HEAD_DIM // 2, 1),
                        pltpu.roll(xc, HEAD_DIM // 2, 1))
    return xc * cos + swapped * sin_signed


def _dup_heads(t, low_half):
    r = pltpu.roll(t, HEAD_DIM, 1)
    return jnp.where(low_half, t, r), jnp.where(low_half, r, t)


def _token_mixing(s, rows, sink_ref, x_ref, cos_ref, sin_ref, g1_ref, win_ref, wpool_ref, pscale_ref,
                  wout_ref, kprev_ref, vprev_ref, uprev_ref, mix_ref, x1_ref):
    T = SEQ_TILE
    x = x_ref[rows, :]
    h = _rmsnorm(x, g1_ref[...]).astype(BF16)
    proj = _dot(h, win_ref[...])
    yield

    lane = lax.broadcasted_iota(jnp.int32, (1, LANES), 1)
    low_half = lane < HEAD_DIM
    first_half = (lane % HEAD_DIM) < (HEAD_DIM // 2)
    cos = cos_ref[rows, :]
    sin_signed = sin_ref[rows, :]

    scale = HEAD_DIM ** -0.5
    q_heads = []
    for c in range(ATTN_WIDTH // LANES):
        qc = _rope(proj[:, c * LANES:(c + 1) * LANES], cos, sin_signed, first_half) * scale
        q_heads.append(jnp.where(low_half, qc, 0.0).astype(BF16))
        q_heads.append(jnp.where(low_half, 0.0, qc).astype(BF16))
    k_r = _rope(proj[:, ATTN_WIDTH:ATTN_WIDTH + KV_WIDTH], cos, sin_signed, first_half)
    k_dup = [t.astype(BF16) for t in _dup_heads(k_r, low_half)]
    k_ext = [jnp.concatenate([kprev_ref[g], k_dup[g]], axis=0) for g in range(N_KV_HEADS)]
    v_t = proj[:, ATTN_WIDTH + KV_WIDTH:ATTN_WIDTH + 2 * KV_WIDTH].T.astype(BF16)
    vt_ext = jnp.concatenate([vprev_ref[...], v_t], axis=1)
    for g in range(N_KV_HEADS):
        kprev_ref[g] = k_dup[g][T - BLOCK:, :]
    vprev_ref[...] = v_t[:, T - BLOCK:]

    kj = lax.broadcasted_iota(jnp.int32, (2 * BLOCK, BLOCK), 0)
    qi = lax.broadcasted_iota(jnp.int32, (2 * BLOCK, BLOCK), 1)
    rel = BLOCK + qi - kj
    band = (rel >= 0) & (rel < WINDOW)
    band_first = band & ((kj >= BLOCK) | (s > 0))
    yield

    chains = [(j, g) for j in range(T // BLOCK) for g in range(N_KV_HEADS)]
    logits, probs = {}, {}
    for j, g in chains:
        blk = slice(j * BLOCK, (j + 1) * BLOCK)
        q_stack = jnp.concatenate([q_heads[g * Q_PER_KV + i][blk, :] for i in range(Q_PER_KV)], axis=0)
        k_win = k_ext[g][j * BLOCK:(j + 2) * BLOCK, :]
        logits[j, g] = lax.dot_general(k_win, q_stack, (((1,), (1,)), ((), ())),
                                       preferred_element_type=F32)
    yield
    for n, (j, g) in enumerate(chains):
        mask1 = band_first if j == 0 else band
        mask = jnp.concatenate([mask1] * Q_PER_KV, axis=1)
        sink = jnp.concatenate([jnp.full((1, BLOCK), sink_ref[g * Q_PER_KV + i], F32)
                                for i in range(Q_PER_KV)], axis=1)
        lg = jnp.where(mask, logits[j, g], -jnp.inf)
        m = jnp.maximum(jnp.max(lg, axis=0, keepdims=True), sink)
        p = jnp.exp(lg - m)
        denom = jnp.sum(p, axis=0, keepdims=True) + jnp.exp(sink - m)
        probs[j, g] = (p * (1.0 / denom)).astype(BF16)
        if n % 2 == 1:
            yield
    for j, g in chains:
        blk = slice(j * BLOCK, (j + 1) * BLOCK)
        vt_win = vt_ext[g * HEAD_DIM:(g + 1) * HEAD_DIM, j * BLOCK:(j + 2) * BLOCK]
        o_t = _dot(vt_win, probs[j, g])
        for cc in range(Q_PER_KV // 2):
            pair = jnp.concatenate([o_t[:, (2 * cc) * BLOCK:(2 * cc + 1) * BLOCK],
                                    o_t[:, (2 * cc + 1) * BLOCK:(2 * cc + 2) * BLOCK]], axis=0)
            c = g * (Q_PER_KV // 2) + cc
            mix_ref[blk, c * LANES:(c + 1) * LANES] = pair.T.astype(BF16)
    yield

    u = proj[:, ATTN_WIDTH + 2 * KV_WIDTH:]
    u_ext = jnp.concatenate([uprev_ref[...], u], axis=0)
    uprev_ref[...] = u[T - POOL_HALO:, :]
    tpos = s * T + lax.broadcasted_iota(jnp.int32, (T, 1), 0)
    pscale = pscale_ref[...]
    for g, w in enumerate(POOL_WINDOWS):
        cols = slice(g * POOL_GROUP_DIM, (g + 1) * POOL_GROUP_DIM)
        acc = u_ext[:, cols]
        step = 1
        while step < w:
            acc = acc + pltpu.roll(acc, step, 0)
            step *= 2
        wsum = acc[POOL_HALO:, :]
        cnt = jnp.minimum(tpos + 1, w).astype(F32)
        d = (wsum / cnt - u[:, cols]).astype(BF16)
        y = _dot(d, wpool_ref[g]) * pscale[:, cols]
        mix_ref[:, ATTN_WIDTH + g * POOL_GROUP_DIM:ATTN_WIDTH + (g + 1) * POOL_GROUP_DIM] = y.astype(BF16)
    yield

    x1_ref[...] = x + _dot(mix_ref[...], wout_ref[...])


def _channel_mixing(rows, x1_ref, g2_ref, wup_ref, wdown_ref, gf_ref, o_ref):
    x1 = x1_ref[...]
    h2 = _rmsnorm(x1, g2_ref[...]).astype(BF16)
    yield
    acc = jnp.zeros((SEQ_TILE, D_MODEL), F32)
    for c in range(D_FF // FF_CHUNK):
        up = _dot(h2, wup_ref[:, c * FF_CHUNK:(c + 1) * FF_CHUNK])
        a = jnp.square(jnp.maximum(up, 0.0)).astype(BF16)
        yield
        acc = acc + _dot(a, wdown_ref[c * FF_CHUNK:(c + 1) * FF_CHUNK, :])
        yield
    x2 = x1 + acc
    o_ref[rows, :] = _rmsnorm(x2, gf_ref[...])


def _block_kernel(steps_per_seq, sink_ref, x_ref, cos_ref, sin_ref, g1_ref, win_ref, wpool_ref,
                  pscale_ref, wout_ref, g2_ref, wup_ref, wdown_ref, gf_ref, o_ref,
                  kprev_ref, vprev_ref, uprev_ref, mix_ref, x1_ref):
    t = pl.program_id(0)
    n_steps = pl.num_programs(0) - 1
    seq_step = jnp.minimum(t, n_steps - 1) % steps_per_seq

    @pl.when(t == 0)
    def _():
        x1_ref[...] = jnp.zeros_like(x1_ref)

    @pl.when(seq_step == 0)
    def _():
        kprev_ref[...] = jnp.zeros_like(kprev_ref)
        vprev_ref[...] = jnp.zeros_like(vprev_ref)
        uprev_ref[...] = jnp.zeros_like(uprev_ref)

    for slot in range(TILES_PER_STEP):
        rows = slice(slot * SEQ_TILE, (slot + 1) * SEQ_TILE)
        s = seq_step * TILES_PER_STEP + slot
        halves = [
            _channel_mixing(rows, x1_ref.at[slot], g2_ref, wup_ref, wdown_ref, gf_ref, o_ref),
            _token_mixing(s, rows, sink_ref, x_ref, cos_ref, sin_ref, g1_ref, win_ref, wpool_ref,
                          pscale_ref, wout_ref, kprev_ref, vprev_ref, uprev_ref, mix_ref,
                          x1_ref.at[slot]),
        ]
        while halves:
            for gen in list(halves):
                if next(gen, StopIteration) is StopIteration:
                    halves.remove(gen)


def _rope_tables(seq):
    half = HEAD_DIM // 2
    inv_freq = ROPE_THETA ** (-np.arange(half, dtype=np.float64) / half)
    ang = np.arange(seq, dtype=np.float64)[:, None] * inv_freq[None, :]
    cos, sin = np.cos(ang), np.sin(ang)
    reps = LANES // HEAD_DIM
    cos_t = np.tile(np.concatenate([cos, cos], axis=1), (1, reps)).astype(np.float32)
    sin_t = np.tile(np.concatenate([-sin, sin], axis=1), (1, reps)).astype(np.float32)
    return jnp.asarray(cos_t), jnp.asarray(sin_t)


def _resident(shape):
    return pl.BlockSpec(shape, lambda t: (0,) * len(shape), pipeline_mode=pl.Buffered(1))


@jax.jit
def kernel(x, attn_norm_g, w_in, attn_sinks, w_pool, pool_scale, w_out, mlp_norm_g, w_up,
           w_down, final_norm_g):
    B, S, D = x.shape
    assert D == D_MODEL and S % STEP_ROWS == 0 and attn_norm_g.shape[0] == 1
    steps_per_seq = S // STEP_ROWS
    n_steps = B * steps_per_seq
    cos_t, sin_t = _rope_tables(S)
    row = lambda v: v.reshape(1, -1).astype(F32)

    def mix_block(t):
        n = jnp.minimum(t, n_steps - 1)
        return n // steps_per_seq, n % steps_per_seq

    def mlp_block(t):
        n = jnp.maximum(t - 1, 0)
        return n // steps_per_seq, n % steps_per_seq

    grid_spec = pl.GridSpec(
        grid=(n_steps + 1,),
        in_specs=[
            pl.BlockSpec(memory_space=pltpu.SMEM),
            pl.BlockSpec((None, STEP_ROWS, D), lambda t: (*mix_block(t), 0)),
            pl.BlockSpec((STEP_ROWS, LANES), lambda t: (mix_block(t)[1], 0)),
            pl.BlockSpec((STEP_ROWS, LANES), lambda t: (mix_block(t)[1], 0)),
            _resident((1, D)),
            _resident((D, IN_WIDTH)),
            _resident((N_POOL_GROUPS, POOL_GROUP_DIM, POOL_GROUP_DIM)),
            _resident((1, POOL_WIDTH)),
            _resident((D, D)),
            _resident((1, D)),
            _resident((D, D_FF)),
            _resident((D_FF, D)),
            _resident((1, D)),
        ],
        out_specs=pl.BlockSpec((None, STEP_ROWS, D), lambda t: (*mlp_block(t), 0)),
        scratch_shapes=[
            pltpu.VMEM((N_KV_HEADS, BLOCK, LANES), BF16),
            pltpu.VMEM((KV_WIDTH, BLOCK), BF16),
            pltpu.VMEM((POOL_HALO, POOL_WIDTH), F32),
            pltpu.VMEM((SEQ_TILE, D), BF16),
            pltpu.VMEM((TILES_PER_STEP, SEQ_TILE, D), F32),
        ],
    )
    return pl.pallas_call(
        functools.partial(_block_kernel, steps_per_seq),
        grid_spec=grid_spec,
        out_shape=jax.ShapeDtypeStruct((B, S, D), x.dtype),
        compiler_params=pltpu.CompilerParams(
            dimension_semantics=("arbitrary",),
            vmem_limit_bytes=VMEM_LIMIT_BYTES,
        ),
        name="hymba_block",
    )(attn_sinks[0].astype(F32), x, cos_t, sin_t, row(attn_norm_g[0]), w_in[0].astype(BF16),
      w_pool[0].astype(BF16), row(pool_scale[0]), w_out[0].astype(BF16), row(mlp_norm_g[0]),
      w_up[0].astype(BF16), w_down[0].astype(BF16), row(final_norm_g))
```
